```python
import math
import jax
import jax.numpy as jnp
from jax import lax
import numpy as np

D_MODEL = 1024
BATCH = 2
SEQ = 16384
DEPTH = 2

N_EVEN = (DEPTH + 1) // 2
N_ODD = DEPTH // 2
DEEPNORM_ALPHA = (2.0 * DEPTH) ** 0.25
DEEPNORM_BETA = (8.0 * DEPTH) ** -0.25
LN_EPS = 1e-5
RMS_EPS = 1e-6
NEG_INF = -1e30

GDN_HEADS = 8
GDN_DK = 64
GDN_DV = 64
GDN_CHUNK = 64
CONV_K = 4
GDN_QK_WIDTH = GDN_HEADS * GDN_DK
GDN_V_WIDTH = GDN_HEADS * GDN_DV
CONV_WIDTH = 2 * GDN_QK_WIDTH + GDN_V_WIDTH
FOX_HEADS = 8
FOX_HEAD_DIM = 64
FOX_WIDTH = FOX_HEADS * FOX_HEAD_DIM
Q_BLOCK = 128
HY_IN_WIDTHS = (CONV_WIDTH, GDN_V_WIDTH, GDN_HEADS, GDN_HEADS, FOX_WIDTH, FOX_WIDTH, FOX_WIDTH, FOX_HEADS)
HY_IN_WIDTH = sum(HY_IN_WIDTHS)
HY_OUT_WIDTH = GDN_V_WIDTH + FOX_WIDTH
MLA_HEADS = 16
MLA_Q_LORA = 512
MLA_KV_LORA = 256
MLA_NOPE = 64
MLA_ROPE = 32
MLA_V = 64
ROPE_THETA = 10000.0
MLA_IN_WIDTHS = (MLA_Q_LORA, MLA_KV_LORA, MLA_ROPE)
MLA_IN_WIDTH = sum(MLA_IN_WIDTHS)
MLA_OUT_WIDTH = MLA_HEADS * MLA_V
N_EXPERTS = 64
N_GROUPS = 8
EXPERTS_PER_GROUP = N_EXPERTS // N_GROUPS
TOP_K = 2
D_FF_EXPERT = 256
MOE_BLOCK = 128
DOC_START_PROB = 1.0 / 2048

kernel_name = 'hybrid_deltanet_fox_mla_grouped_moe'


def split_cols(t, widths):
    cuts = [int(c) for c in np.cumsum(widths)[:-1]]
    return jnp.split(t, cuts, axis=-1)


def layer_norm(x, gain, bias):
    xf = x.astype(jnp.float32)
    mu = jnp.mean(xf, axis=-1, keepdims=True)
    var = jnp.mean(jnp.square(xf - mu), axis=-1, keepdims=True)
    y = (xf - mu) * lax.rsqrt(var + LN_EPS) * gain.astype(jnp.float32) + bias.astype(jnp.float32)
    return y.astype(x.dtype)


def rms_norm(x, gain):
    xf = x.astype(jnp.float32)
    y = xf * lax.rsqrt(jnp.mean(jnp.square(xf), axis=-1, keepdims=True) + RMS_EPS) * gain.astype(jnp.float32)
    return y.astype(x.dtype)


def l2_normalize(x):
    xf = x.astype(jnp.float32)
    return xf * lax.rsqrt(jnp.sum(jnp.square(xf), axis=-1, keepdims=True) + RMS_EPS)


def causal_depthwise_conv(u, w):
    return lax.conv_general_dilated(
        u, w[:, None, :].astype(u.dtype), window_strides=(1,), padding=[(CONV_K - 1, 0)],
        dimension_numbers=('NWC', 'WIO', 'NWC'), feature_group_count=u.shape[-1])


def rotary(t, positions):
    inv_freq = ROPE_THETA ** (-jnp.arange(0, MLA_ROPE, 2, dtype=jnp.float32) / MLA_ROPE)
    ang = positions.astype(jnp.float32)[..., None] * inv_freq
    ang = ang.reshape(ang.shape[:2] + (1,) * (t.ndim - 3) + ang.shape[-1:])
    cos, sin = jnp.cos(ang), jnp.sin(ang)
    t1, t2 = jnp.split(t.astype(jnp.float32), 2, axis=-1)
    return jnp.concatenate([t1 * cos - t2 * sin, t1 * sin + t2 * cos], axis=-1).astype(t.dtype)


def block_causal_attention(q, k, v, log_forget_cum=None):
    b, s, h, dqk = q.shape
    nb = s // Q_BLOCK
    scale = dqk ** -0.5
    key_pos = jnp.arange(s)
    q_blocks = jnp.swapaxes(q.reshape(b, nb, Q_BLOCK, h, dqk), 0, 1)
    xs = (jnp.arange(nb), q_blocks)
    if log_forget_cum is not None:
        c_keys = jnp.transpose(log_forget_cum, (0, 2, 1)).astype(jnp.float32)
        c_blocks = jnp.swapaxes(log_forget_cum.reshape(b, nb, Q_BLOCK, h), 0, 1)
        xs = xs + (c_blocks,)

    def one_block(args):
        blk, q_blk = args[0], args[1]
        logits = jnp.einsum('bqhd,bkhd->bhqk', q_blk, k, preferred_element_type=jnp.float32) * scale
        if log_forget_cum is not None:
            c_q = jnp.transpose(args[2], (0, 2, 1)).astype(jnp.float32)
            logits = logits + (c_q[..., :, None] - c_keys[:, :, None, :])
        query_pos = blk * Q_BLOCK + jnp.arange(Q_BLOCK)
        causal = key_pos[None, :] <= query_pos[:, None]
        logits = jnp.where(causal, logits, NEG_INF)
        probs = jax.nn.softmax(logits, axis=-1).astype(v.dtype)
        return jnp.einsum('bhqk,bkhd->bqhd', probs, v)

    out = lax.map(one_block, xs)
    return jnp.swapaxes(out, 0, 1).reshape(b, s, h, v.shape[-1])


def chunked_gated_delta_rule(q, k, v, log_decay, beta):
    b, s, h, dk = q.shape
    dv = v.shape[-1]
    n = s // GDN_CHUNK
    f32 = jnp.float32

    def chunks(t):
        return jnp.transpose(t.astype(f32).reshape(b, n, GDN_CHUNK, h, -1), (0, 3, 1, 2, 4))

    def chunks_scalar(t):
        return jnp.transpose(t.astype(f32).reshape(b, n, GDN_CHUNK, h), (0, 3, 1, 2))

    qc = chunks(q) * dk ** -0.5
    kc = chunks(k)
    vc = chunks(v)
    bc = chunks_scalar(beta)
    g = jnp.cumsum(chunks_scalar(log_decay), axis=-1)
    incl = jnp.tril(jnp.ones((GDN_CHUNK, GDN_CHUNK), dtype=bool))
    strict = jnp.tril(jnp.ones((GDN_CHUNK, GDN_CHUNK), dtype=bool), -1)
    gdiff = g[..., :, None] - g[..., None, :]
    decay = jnp.where(incl, jnp.exp(jnp.where(incl, gdiff, 0.0)), 0.0)
    k_beta = kc * bc[..., None]
    v_beta = vc * bc[..., None]
    lower = jnp.where(strict, jnp.einsum('bhnid,bhnjd->bhnij', k_beta, kc) * decay, 0.0)
    eye = jnp.eye(GDN_CHUNK, dtype=f32)
    t_inv = lax.linalg.triangular_solve(eye + lower, jnp.broadcast_to(eye, lower.shape),
                                        left_side=True, lower=True, unit_diagonal=True)
    u = jnp.einsum('bhnij,bhnjd->bhnid', t_inv, v_beta)
    w = jnp.einsum('bhnij,bhnjd->bhnid', t_inv, k_beta * jnp.exp(g)[..., None])
    attn = jnp.where(incl, jnp.einsum('bhnid,bhnjd->bhnij', qc, kc) * decay, 0.0)
    q_dec = qc * jnp.exp(g)[..., None]
    g_last = g[..., -1]
    k_tail = kc * jnp.exp(g_last[..., None] - g)[..., None]

    def step(state, xs):
        q_i, w_i, u_i, a_i, k_i, gl_i = xs
        v_new = u_i - jnp.einsum('bhck,bhkv->bhcv', w_i, state)
        o_i = jnp.einsum('bhck,bhkv->bhcv', q_i, state) + jnp.einsum('bhcj,bhjv->bhcv', a_i, v_new)
        state = state * jnp.exp(gl_i)[..., None, None] + jnp.einsum('bhck,bhcv->bhkv', k_i, v_new)
        return state, o_i

    xs = tuple(jnp.moveaxis(t, 2, 0) for t in (q_dec, w, u, attn, k_tail, g_last))
    state0 = jnp.zeros((b, h, dk, dv), f32)
    _, out = lax.scan(step, state0, xs)
    return jnp.transpose(out, (1, 0, 3, 2, 4)).reshape(b, s, h, dv).astype(v.dtype)


def delta_fox_mixer(x, w_in, conv_w, a_log, dt_bias, norm_w, f_bias, w_out):
    b, s, _ = x.shape
    f32 = jnp.float32
    qkv_d, z, beta_logit, a_logit, q_f, k_f, v_f, f_logit = split_cols(x @ w_in, HY_IN_WIDTHS)
    qkv_d = jax.nn.silu(causal_depthwise_conv(qkv_d, conv_w))
    q_d, k_d, v_d = split_cols(qkv_d, (GDN_QK_WIDTH, GDN_QK_WIDTH, GDN_V_WIDTH))
    q_d = l2_normalize(q_d.reshape(b, s, GDN_HEADS, GDN_DK))
    k_d = l2_normalize(k_d.reshape(b, s, GDN_HEADS, GDN_DK))
    v_d = v_d.reshape(b, s, GDN_HEADS, GDN_DV)
    beta = jax.nn.sigmoid(beta_logit.astype(f32))
    log_decay = -jnp.exp(a_log.astype(f32)) * jax.nn.softplus(a_logit.astype(f32) + dt_bias.astype(f32))
    o_d = chunked_gated_delta_rule(q_d, k_d, v_d, log_decay, beta)
    o_d = rms_norm(o_d, norm_w) * jax.nn.silu(z.reshape(b, s, GDN_HEADS, GDN_DV))
    log_f = jax.nn.log_sigmoid(f_logit.astype(f32) + f_bias.astype(f32))
    o_f = block_causal_attention(q_f.reshape(b, s, FOX_HEADS, FOX_HEAD_DIM),
                                 k_f.reshape(b, s, FOX_HEADS, FOX_HEAD_DIM),
                                 v_f.reshape(b, s, FOX_HEADS, FOX_HEAD_DIM),
                                 jnp.cumsum(log_f, axis=1))
    mixed = jnp.concatenate([o_d.reshape(b, s, GDN_V_WIDTH), o_f.reshape(b, s, FOX_WIDTH)], axis=-1)
    return mixed @ w_out


def mla_mixer(x, positions, w_in, q_norm_w, kv_norm_w, w_uq, w_ukv, w_out):
    b, s, _ = x.shape
    c_q, c_kv, k_rope = split_cols(x @ w_in, MLA_IN_WIDTHS)
    q = (rms_norm(c_q, q_norm_w) @ w_uq).reshape(b, s, MLA_HEADS, MLA_NOPE + MLA_ROPE)
    kv = (rms_norm(c_kv, kv_norm_w) @ w_ukv).reshape(b, s, MLA_HEADS, MLA_NOPE + MLA_V)
    q = jnp.concatenate([q[..., :MLA_NOPE], rotary(q[..., MLA_NOPE:], positions)], axis=-1)
    k_rope = jnp.broadcast_to(rotary(k_rope, positions)[:, :, None, :], (b, s, MLA_HEADS, MLA_ROPE))
    k = jnp.concatenate([kv[..., :MLA_NOPE], k_rope], axis=-1)
    o = block_causal_attention(q, k, kv[..., MLA_NOPE:])
    return o.reshape(b, s, MLA_OUT_WIDTH) @ w_out


def grouped_moe(x, router_w, router_bias, w_gate, w_up, w_down):
    b, s, d = x.shape
    n_tok = b * s
    n_pairs = n_tok * TOP_K
    n_blocks = n_pairs // MOE_BLOCK + N_EXPERTS
    xt = x.reshape(n_tok, d)
    scores = jax.nn.sigmoid(jnp.einsum('td,de->te', xt, router_w, preferred_element_type=jnp.float32))
    biased = (scores + router_bias.astype(jnp.float32)).reshape(n_tok, N_GROUPS, EXPERTS_PER_GROUP)
    group = jnp.argmax(jnp.sum(lax.top_k(biased, TOP_K)[0], axis=-1), axis=-1)
    in_group = jnp.take_along_axis(biased, group[:, None, None], axis=1)[:, 0]
    local = lax.top_k(in_group, TOP_K)[1]
    expert_idx = group[:, None] * EXPERTS_PER_GROUP + local
    chosen = jnp.take_along_axis(scores, expert_idx, axis=-1)
    gates = chosen / jnp.sum(chosen, axis=-1, keepdims=True)
    pair_expert = expert_idx.reshape(n_pairs)
    order = jnp.argsort(pair_expert)
    e_sorted = pair_expert[order]
    tok_sorted = order // TOP_K
    gate_sorted = gates.reshape(n_pairs)[order]
    counts = jnp.bincount(pair_expert, length=N_EXPERTS)
    padded = (counts + MOE_BLOCK - 1) // MOE_BLOCK * MOE_BLOCK
    padded_end = jnp.cumsum(padded)
    rank = jnp.arange(n_pairs) - (jnp.cumsum(counts) - counts)[e_sorted]
    dest = (padded_end - padded)[e_sorted] + rank
    rows = jnp.zeros((n_blocks * MOE_BLOCK, d), x.dtype).at[dest].set(xt[tok_sorted])
    block_expert = jnp.minimum(
        jnp.searchsorted(padded_end, jnp.arange(n_blocks) * MOE_BLOCK, side='right'), N_EXPERTS - 1)

    def expert_block(args):
        r, e = args
        hidden = jax.nn.silu(r @ w_gate[e]) * (r @ w_up[e])
        return hidden @ w_down[e]

    out_rows = lax.map(expert_block, (rows.reshape(n_blocks, MOE_BLOCK, d), block_expert)).reshape(-1, d)
    y = jnp.zeros((n_tok, d), x.dtype).at[tok_sorted].add(out_rows[dest] * gate_sorted[:, None].astype(x.dtype))
    return y.reshape(b, s, d)


def setup_inputs(seed: int = 0) -> dict:
    key = jax.random.key(seed)
    ks = jax.random.split(key, 24)
    f32 = jnp.float32

    def nrm(k, shape, scale):
        return jax.random.normal(k, shape, f32) * scale

    x = jax.random.normal(ks[0], (BATCH, SEQ, D_MODEL), f32)
    starts = jax.random.bernoulli(ks[1], DOC_START_PROB, (BATCH, SEQ)).at[:, 0].set(True)
    t = jnp.arange(SEQ, dtype=jnp.int32)
    doc_start = lax.cummax(jnp.where(starts, t[None, :], 0), axis=1)
    positions = (t[None, :] - doc_start).astype(jnp.int32)
    ln_gain = 1.0 + nrm(ks[2], (DEPTH, 2, D_MODEL), 0.02)
    ln_bias = nrm(ks[3], (DEPTH, 2, D_MODEL), 0.02)
    router_w = nrm(ks[4], (D_MODEL, N_EXPERTS), D_MODEL ** -0.5)
    router_bias = nrm(ks[5], (N_EXPERTS,), 0.01)
    moe_w_gate = nrm(ks[6], (DEPTH, N_EXPERTS, D_MODEL, D_FF_EXPERT), D_MODEL ** -0.5)
    moe_w_up = nrm(ks[7], (DEPTH, N_EXPERTS, D_MODEL, D_FF_EXPERT), D_MODEL ** -0.5)
    moe_w_down = nrm(ks[8], (DEPTH, N_EXPERTS, D_FF_EXPERT, D_MODEL), D_FF_EXPERT ** -0.5 * DEEPNORM_BETA)
    hy_w_in = nrm(ks[9], (N_EVEN, D_MODEL, HY_IN_WIDTH), D_MODEL ** -0.5)
    hy_conv_w = nrm(ks[10], (N_EVEN, CONV_K, CONV_WIDTH), CONV_K ** -0.5)
    gdn_a_log = jnp.log(jax.random.uniform(ks[11], (N_EVEN, GDN_HEADS), f32, minval=1.0, maxval=16.0))
    dt = jnp.exp(jax.random.uniform(ks[12], (N_EVEN, GDN_HEADS), f32,
                                    minval=math.log(1e-3), maxval=math.log(1e-1)))
    gdn_dt_bias = dt + jnp.log(-jnp.expm1(-dt))
    gdn_norm_w = 1.0 + nrm(ks[13], (N_EVEN, GDN_DV), 0.02)
    fox_f_bias = jnp.linspace(1.0, 6.0, FOX_HEADS, dtype=f32)[None, :] + nrm(ks[14], (N_EVEN, FOX_HEADS), 0.1)
    hy_w_out = nrm(ks[15], (N_EVEN, HY_OUT_WIDTH, D_MODEL), HY_OUT_WIDTH ** -0.5 * DEEPNORM_BETA)
    mla_w_in = nrm(ks[16], (N_ODD, D_MODEL, MLA_IN_WIDTH), D_MODEL ** -0.5)
    mla_q_norm = 1.0 + nrm(ks[17], (N_ODD, MLA_Q_LORA), 0.02)
    mla_kv_norm = 1.0 + nrm(ks[18], (N_ODD, MLA_KV_LORA), 0.02)
    mla_w_uq = nrm(ks[19], (N_ODD, MLA_Q_LORA, MLA_HEADS * (MLA_NOPE + MLA_ROPE)), MLA_Q_LORA ** -0.5)
    mla_w_ukv = nrm(ks[20], (N_ODD, MLA_KV_LORA, MLA_HEADS * (MLA_NOPE + MLA_V)), MLA_KV_LORA ** -0.5)
    mla_w_out = nrm(ks[21], (N_ODD, MLA_OUT_WIDTH, D_MODEL), MLA_OUT_WIDTH ** -0.5 * DEEPNORM_BETA)
    return {'x': x, 'positions': positions, 'ln_gain': ln_gain, 'ln_bias': ln_bias,
            'router_w': router_w, 'router_bias': router_bias,
            'moe_w_gate': moe_w_gate, 'moe_w_up': moe_w_up, 'moe_w_down': moe_w_down,
            'hy_w_in': hy_w_in, 'hy_conv_w': hy_conv_w, 'gdn_a_log': gdn_a_log,
            'gdn_dt_bias': gdn_dt_bias, 'gdn_norm_w': gdn_norm_w, 'fox_f_bias': fox_f_bias,
            'hy_w_out': hy_w_out, 'mla_w_in': mla_w_in, 'mla_q_norm': mla_q_norm,
            'mla_kv_norm': mla_kv_norm, 'mla_w_uq': mla_w_uq, 'mla_w_ukv': mla_w_ukv,
            'mla_w_out': mla_w_out}


def reference(x, positions, ln_gain, ln_bias, router_w, router_bias, moe_w_gate, moe_w_up, moe_w_down,
              hy_w_in, hy_conv_w, gdn_a_log, gdn_dt_bias, gdn_norm_w, fox_f_bias, hy_w_out,
              mla_w_in, mla_q_norm, mla_kv_norm, mla_w_uq, mla_w_ukv, mla_w_out):
    for layer in range(DEPTH):
        j = layer // 2
        if layer % 2 == 0:
            mixed = delta_fox_mixer(x, hy_w_in[j], hy_conv_w[j], gdn_a_log[j], gdn_dt_bias[j],
                                    gdn_norm_w[j], fox_f_bias[j], hy_w_out[j])
        else:
            mixed = mla_mixer(x, positions, mla_w_in[j], mla_q_norm[j], mla_kv_norm[j],
                              mla_w_uq[j], mla_w_ukv[j], mla_w_out[j])
        x = layer_norm(DEEPNORM_ALPHA * x + mixed, ln_gain[layer, 0], ln_bias[layer, 0])
        ffn = grouped_moe(x, router_w, router_bias, moe_w_gate[layer], moe_w_up[layer], moe_w_down[layer])
        x = layer_norm(DEEPNORM_ALPHA * x + ffn, ln_gain[layer, 1], ln_bias[layer, 1])
    return x
```

```python
import functools
import math

import numpy as np
import jax
import jax.numpy as jnp
from jax import lax
from jax.experimental import pallas as pl
from jax.experimental.pallas import tpu as pltpu

F32 = jnp.float32
BF16 = jnp.bfloat16

D_MODEL = 1024
LN_EPS = 1e-5
RMS_EPS = 1e-6
NEG_INF = -1e30
LANES = 128

GDN_HEADS = 8
GDN_D = 64
GDN_CHUNK = 64
CONV_K = 4
GDN_WIDTH = GDN_HEADS * GDN_D
FOX_HEADS = 8
FOX_D = 64
FOX_WIDTH = FOX_HEADS * FOX_D
MLA_HEADS = 16
MLA_Q_LORA = 512
MLA_KV_LORA = 256
MLA_NOPE = 64
MLA_ROPE = 32
MLA_V = 64
ROPE_THETA = 10000.0
N_EXPERTS = 64
N_GROUPS = 8
EXPERTS_PER_GROUP = 8
TOP_K = 2
D_FF_EXPERT = 256

VMEM_LIMIT = 56 * 1024 * 1024
HIGHEST = lax.Precision.HIGHEST


def _cparams(*sem):
    return pltpu.CompilerParams(dimension_semantics=sem, vmem_limit_bytes=VMEM_LIMIT)


def _dot(a, b, precision=None):
    return jnp.dot(a, b, preferred_element_type=F32, precision=precision)


def _dot_nt(a, b, precision=None):
    return lax.dot_general(a, b, (((1,), (1,)), ((), ())), preferred_element_type=F32, precision=precision)


def _dot_tn(a, b, precision=None):
    return lax.dot_general(a, b, (((0,), (0,)), ((), ())), preferred_element_type=F32, precision=precision)


def _silu(x):
    return x * jax.nn.sigmoid(x)


def _layer_norm(r, gain, bias):
    mu = jnp.mean(r, axis=-1, keepdims=True)
    d = r - mu
    var = jnp.mean(d * d, axis=-1, keepdims=True)
    return d * lax.rsqrt(var + LN_EPS) * gain + bias


def _rms_norm(x, gain):
    return x * lax.rsqrt(jnp.mean(x * x, axis=-1, keepdims=True) + RMS_EPS) * gain


def _proj_kernel(x_ref, *refs):
    n = len(refs) // 2
    x = x_ref[...].astype(BF16)
    for w_ref, o_ref in zip(refs[:n], refs[n:]):
        o_ref[...] = _dot(x, w_ref[...]).astype(o_ref.dtype)


def _proj(x, ws, out_dtypes, tm):
    m, k = x.shape
    in_specs = [pl.BlockSpec((tm, k), lambda i: (i, 0))]
    in_specs += [pl.BlockSpec(w.shape, lambda i: (0, 0)) for w in ws]
    out_specs = [pl.BlockSpec((tm, w.shape[1]), lambda i: (i, 0)) for w in ws]
    out_shape = [jax.ShapeDtypeStruct((m, w.shape[1]), dt) for w, dt in zip(ws, out_dtypes)]
    return pl.pallas_call(
        _proj_kernel, grid=(m // tm,), in_specs=in_specs, out_specs=out_specs, out_shape=out_shape,
        compiler_params=_cparams("parallel"), name="proj")(x, *ws)


def _out_ln_kernel(*refs, n_in, alpha):
    a_refs = refs[:n_in]
    w_refs = refs[n_in:2 * n_in]
    x_ref, g_ref, b_ref, o_ref = refs[2 * n_in:]
    y = None
    for a_ref, w_ref in zip(a_refs, w_refs):
        t = _dot(a_ref[...].astype(BF16), w_ref[...])
        y = t if y is None else y + t
    r = alpha * x_ref[...] + y
    o_ref[...] = _layer_norm(r, g_ref[...], b_ref[...])


def _out_ln(acts, ws, x, gain, bias, alpha, tm):
    m, d = x.shape
    n_in = len(acts)
    in_specs = [pl.BlockSpec((tm, a.shape[1]), lambda i: (i, 0)) for a in acts]
    in_specs += [pl.BlockSpec(w.shape, lambda i: (0, 0)) for w in ws]
    in_specs += [pl.BlockSpec((tm, d), lambda i: (i, 0)),
                 pl.BlockSpec((1, d), lambda i: (0, 0)), pl.BlockSpec((1, d), lambda i: (0, 0))]
    return pl.pallas_call(
        functools.partial(_out_ln_kernel, n_in=n_in, alpha=alpha),
        grid=(m // tm,), in_specs=in_specs, out_specs=pl.BlockSpec((tm, d), lambda i: (i, 0)),
        out_shape=jax.ShapeDtypeStruct((m, d), F32),
        compiler_params=_cparams("parallel"), name="out_ln")(*acts, *ws, x, gain.reshape(1, d), bias.reshape(1, d))


def _attn_kernel(*refs, scale, tq, tk, packed, has_bias):
    if has_bias:
        q_ref, k_ref, v_ref, cq_ref, ck_ref, o_ref, m_sc, l_sc, acc_sc, cq_sc = refs
    else:
        q_ref, k_ref, v_ref, o_ref, m_sc, l_sc, acc_sc = refs
    p_id = pl.program_id(1)
    i = pl.program_id(2)
    j = pl.program_id(3)
    last_j = ((i + 1) * tq - 1) // tk

    @pl.when(j == 0)
    def _init():
        m_sc[...] = jnp.full(m_sc.shape, NEG_INF, F32)
        l_sc[...] = jnp.zeros(l_sc.shape, F32)
        acc_sc[...] = jnp.zeros(acc_sc.shape, F32)
        if has_bias:
            lane = lax.broadcasted_iota(jnp.int32, (tq, LANES), 1)
            cq = cq_ref[...]
            for hh in range(2):
                col = jnp.sum(jnp.where(lane == 2 * p_id + hh, cq, 0.0), axis=1, keepdims=True)
                cq_sc[hh] = jnp.broadcast_to(col, (tq, LANES))

    def _step(masked):
        v = v_ref[...]
        if packed:
            q2 = q_ref[...]
            k2 = k_ref[...]
            qlane = lax.broadcasted_iota(jnp.int32, q2.shape, 1)
        if masked:
            row = i * tq + lax.broadcasted_iota(jnp.int32, (tq, tk), 0)
            col = j * tk + lax.broadcasted_iota(jnp.int32, (tq, tk), 1)
            causal = col <= row
        for hh in range(2):
            if packed:
                q = jnp.where((qlane >= 64 * hh) & (qlane < 64 * (hh + 1)), q2, jnp.zeros_like(q2))
                k = k2
            else:
                q = q_ref[:, hh * LANES:(hh + 1) * LANES]
                k = k_ref[:, hh * LANES:(hh + 1) * LANES]
            s = _dot_nt(q, k) * scale
            if has_bias:
                s = s - ck_ref[0, 0, hh:hh + 1, :]
            if masked:
                s = jnp.where(causal, s, NEG_INF)
            m_prev = m_sc[hh]
            m_cur = jnp.max(s, axis=1, keepdims=True)
            if has_bias:
                cqc = cq_sc[hh][:, :1]
                m_cur = m_cur + cqc
            m_new = jnp.maximum(m_prev, m_cur)
            alpha = jnp.exp(m_prev - m_new)
            shift = m_new[:, :1] - cqc if has_bias else m_new[:, :1]
            p = jnp.exp(s - shift)
            l_sc[hh] = alpha * l_sc[hh] + jnp.sum(p, axis=1, keepdims=True)
            acc_sc[hh] = alpha * acc_sc[hh] + _dot(p.astype(BF16), v)
            m_sc[hh] = m_new

    below_diagonal = (j + 1) * tk <= i * tq + 1

    @pl.when(below_diagonal)
    def _off_diag():
        _step(False)

    @pl.when(jnp.logical_not(below_diagonal) & (j <= last_j))
    def _diag():
        _step(True)

    @pl.when(j == last_j)
    def _finish():
        lane = lax.broadcasted_iota(jnp.int32, (tq, LANES), 1)
        o0 = acc_sc[0] / l_sc[0]
        o1 = acc_sc[1] / l_sc[1]
        o_ref[...] = jnp.where(lane < 64, o0, o1).astype(o_ref.dtype)


def _attention(q_arr, q_off, k_arr, k_off, v_arr, v_off, n_heads, batch, seq, scale, packed, out_dtype,
               cq=None, ck=None, tq=512, tk=512):
    t = batch * seq
    hp = n_heads // 2
    qk_w = LANES if packed else 2 * LANES
    nq = seq // tq
    nk = seq // tk
    has_bias = cq is not None

    def last_needed(i):
        return ((i + 1) * tq - 1) // tk

    def q_map(b, p, i, j):
        return (b * nq + i, q_off + p)

    def k_map(b, p, i, j):
        return (b * nk + jnp.minimum(j, last_needed(i)), k_off + p)

    def v_map(b, p, i, j):
        return (b * nk + jnp.minimum(j, last_needed(i)), v_off + p)

    in_specs = [pl.BlockSpec((tq, qk_w), q_map), pl.BlockSpec((tk, qk_w), k_map), pl.BlockSpec((tk, LANES), v_map)]
    args = [q_arr, k_arr, v_arr]
    scratch = [pltpu.VMEM((2, tq, LANES), F32), pltpu.VMEM((2, tq, LANES), F32), pltpu.VMEM((2, tq, LANES), F32)]
    if has_bias:
        in_specs += [pl.BlockSpec((tq, LANES), lambda b, p, i, j: (b * nq + i, 0)),
                     pl.BlockSpec((1, 1, 2, tk), lambda b, p, i, j: (b, p, 0, jnp.minimum(j, last_needed(i))))]
        args += [cq, ck]
        scratch += [pltpu.VMEM((2, tq, LANES), F32)]
    return pl.pallas_call(
        functools.partial(_attn_kernel, scale=scale, tq=tq, tk=tk, packed=packed, has_bias=has_bias),
        grid=(batch, hp, nq, nk), in_specs=in_specs,
        out_specs=pl.BlockSpec((tq, LANES), lambda b, p, i, j: (b * nq + i, p)),
        out_shape=jax.ShapeDtypeStruct((t, hp * LANES), out_dtype),
        scratch_shapes=scratch,
        compiler_params=_cparams("parallel", "parallel", "parallel", "arbitrary"), name="attention")(*args)


def _fox_gate_kernel(s_ref, fb_ref, cq_ref, ck_ref, carry, *, tm):
    @pl.when(pl.program_id(1) == 0)
    def _():
        carry[...] = jnp.zeros(carry.shape, F32)

    logf = jax.nn.log_sigmoid(s_ref[...] + fb_ref[...])
    row = lax.broadcasted_iota(jnp.int32, (tm, tm), 0)
    col = lax.broadcasted_iota(jnp.int32, (tm, tm), 1)
    tri = (col <= row).astype(F32)
    c = _dot(tri, logf, HIGHEST) + carry[0:1, :]
    carry[...] = jnp.broadcast_to(c[tm - 1:tm, :], carry.shape)
    cq_ref[...] = c
    ck_ref[0] = c.T[0:8, :]


def _fox_gate(small, f_bias_row, batch, seq, tm=256):
    t = batch * seq
    n = seq // tm
    return pl.pallas_call(
        functools.partial(_fox_gate_kernel, tm=tm), grid=(batch, n),
        in_specs=[pl.BlockSpec((tm, LANES), lambda b, i: (b * n + i, 0)), pl.BlockSpec((1, LANES), lambda b, i: (0, 0))],
        out_specs=[pl.BlockSpec((tm, LANES), lambda b, i: (b * n + i, 0)), pl.BlockSpec((1, 8, tm), lambda b, i: (b, 0, i))],
        out_shape=[jax.ShapeDtypeStruct((t, LANES), F32), jax.ShapeDtypeStruct((batch, 8, seq), F32)],
        scratch_shapes=[pltpu.VMEM((8, LANES), F32)],
        compiler_params=_cparams("parallel", "arbitrary"), name="fox_gate")(small, f_bias_row)


def _gdn_local_kernel(x_ref, halo_ref, s_ref, cw_ref, alog_ref, dtb_ref,
                      qe_ref, w_ref, u_ref, kt_ref, ol_ref, dl_ref,
                      y_sc, beta_sc, ld_sc, *, tm, tiles_per_batch):
    c_sz = GDN_CHUNK
    n_chunks = tm // c_sz
    first = (pl.program_id(0) % tiles_per_batch) == 0
    halo = jnp.where(first, 0.0, halo_ref[...])
    ext = jnp.concatenate([halo, x_ref[...]], axis=0)
    acc = None
    for jj in range(CONV_K):
        term = cw_ref[jj:jj + 1, :] * ext[8 - (CONV_K - 1) + jj: 8 - (CONV_K - 1) + jj + tm, :]
        acc = term if acc is None else acc + term
    y_sc[...] = _silu(acc)
    sm = s_ref[...]
    beta_sc[...] = jax.nn.sigmoid(sm)
    ld_sc[...] = -jnp.exp(alog_ref[...]) * jax.nn.softplus(sm + dtb_ref[...])

    r64 = lax.broadcasted_iota(jnp.int32, (c_sz, c_sz), 0)
    c64 = lax.broadcasted_iota(jnp.int32, (c_sz, c_sz), 1)
    tri64 = (c64 <= r64).astype(F32)
    lane64 = lax.broadcasted_iota(jnp.int32, (c_sz, LANES), 1) < 64
    rowp = lax.broadcasted_iota(jnp.int32, (LANES, LANES), 0)
    colp = lax.broadcasted_iota(jnp.int32, (LANES, LANES), 1)
    row_h1 = rowp >= 64
    same_head = row_h1 == (colp >= 64)
    ti = rowp % 64
    tj = colp % 64
    incl = same_head & (tj <= ti)
    strict = same_head & (tj < ti)
    eye = (rowp == colp).astype(F32)
    lane_h1_full = lax.broadcasted_iota(jnp.int32, (LANES, LANES), 1) >= 64
    scale = GDN_D ** -0.5

    def sel(a0, a1):
        return jnp.where(lane64, a0, a1)

    def chunk_body(c, carry):
        r0 = pl.multiple_of(c * c_sz, c_sz)
        rows = pl.ds(r0, c_sz)
        bl = beta_sc[rows, :]
        g = _dot(tri64, ld_sc[rows, :], HIGHEST)
        g_t = jnp.concatenate([g, g], axis=0).T
        dl_row = []
        for p in range(GDN_HEADS // 2):
            h0, h1 = 2 * p, 2 * p + 1
            q2 = y_sc[rows, p * LANES:(p + 1) * LANES]
            k2 = y_sc[rows, GDN_WIDTH + p * LANES:GDN_WIDTH + (p + 1) * LANES]
            v2 = y_sc[rows, 2 * GDN_WIDTH + p * LANES:2 * GDN_WIDTH + (p + 1) * LANES]

            def l2n(x2):
                sq = x2 * x2
                s0 = jnp.sum(jnp.where(lane64, sq, 0.0), axis=1, keepdims=True)
                s1 = jnp.sum(jnp.where(lane64, 0.0, sq), axis=1, keepdims=True)
                return x2 * sel(lax.rsqrt(s0 + RMS_EPS), lax.rsqrt(s1 + RMS_EPS))

            kn2 = l2n(k2)
            qs2 = l2n(q2) * scale
            bcol2 = sel(bl[:, 8 + h0:9 + h0], bl[:, 8 + h1:9 + h1])
            g0c = g[:, 16 + h0:17 + h0]
            g1c = g[:, 16 + h1:17 + h1]
            gcol2 = sel(g0c, g1c)
            eg2 = jnp.exp(gcol2)
            kb2 = kn2 * bcol2
            vb2 = v2 * bcol2
            kbg2 = kb2 * eg2
            qdec2 = qs2 * eg2
            gl0 = g[c_sz - 1:c_sz, 16 + h0:17 + h0]
            gl1 = g[c_sz - 1:c_sz, 16 + h1:17 + h1]
            glast2 = sel(gl0, gl1)
            ktail2 = kn2 * jnp.exp(glast2 - gcol2)
            dl_row.append(jnp.exp(glast2[0:1, :]))

            gcol_p = jnp.concatenate([jnp.broadcast_to(g0c, (c_sz, LANES)), jnp.broadcast_to(g1c, (c_sz, LANES))], axis=0)
            grow_p = jnp.where(row_h1, g_t[16 + h1:17 + h1, :], g_t[16 + h0:17 + h0, :])
            gdiff = gcol_p - grow_p
            decay = jnp.where(incl, jnp.exp(jnp.where(incl, gdiff, 0.0)), 0.0)

            kk = jnp.concatenate([kn2, kn2], axis=0).astype(BF16)
            kbm = jnp.concatenate([jnp.where(lane64, kb2, 0.0), jnp.where(lane64, 0.0, kb2)], axis=0).astype(BF16)
            qm = jnp.concatenate([jnp.where(lane64, qs2, 0.0), jnp.where(lane64, 0.0, qs2)], axis=0).astype(BF16)
            lower = jnp.where(strict, _dot_nt(kbm, kk) * decay, 0.0)
            attn = jnp.where(incl, _dot_nt(qm, kk) * decay, 0.0)

            mpow = -lower
            tinv = eye + mpow
            for _ in range(5):
                mpow = _dot(mpow, mpow, HIGHEST)
                tinv = tinv + _dot(tinv, mpow, HIGHEST)

            rhs = jnp.concatenate([jnp.concatenate([vb2, vb2], axis=0), jnp.concatenate([kbg2, kbg2], axis=0)], axis=1)
            uw = _dot(tinv.astype(BF16), rhs.astype(BF16))
            olq = _dot(attn.astype(BF16), uw.astype(BF16))
            u2 = sel(uw[:c_sz, :LANES], uw[c_sz:, :LANES])
            w2 = sel(uw[:c_sz, LANES:], uw[c_sz:, LANES:])
            ol2 = sel(olq[:c_sz, :LANES], olq[c_sz:, :LANES])
            aw2 = sel(olq[:c_sz, LANES:], olq[c_sz:, LANES:])
            cols = slice(p * LANES, (p + 1) * LANES)
            qe_ref[rows, cols] = qdec2 - aw2
            w_ref[rows, cols] = w2
            u_ref[rows, cols] = u2
            kt_ref[rows, cols] = ktail2
            ol_ref[rows, cols] = ol2
        dl_ref[pl.ds(c, 1), :] = jnp.concatenate(dl_row, axis=1)
        return carry

    lax.fori_loop(0, n_chunks, chunk_body, 0)


def _gdn_local(qkv_pre, small, conv_w, alog_row, dtb_row, batch, seq, tm=512):
    t = batch * seq
    w3 = 3 * GDN_WIDTH
    tiles_per_batch = seq // tm
    tok = lambda i: (i, 0)
    outs = [jax.ShapeDtypeStruct((t, GDN_WIDTH), F32)] * 5 + [jax.ShapeDtypeStruct((t // GDN_CHUNK, GDN_WIDTH), F32)]
    out_specs = [pl.BlockSpec((tm, GDN_WIDTH), tok)] * 5 + [pl.BlockSpec((tm // GDN_CHUNK, GDN_WIDTH), tok)]
    return pl.pallas_call(
        functools.partial(_gdn_local_kernel, tm=tm, tiles_per_batch=tiles_per_batch),
        grid=(t // tm,),
        in_specs=[pl.BlockSpec((tm, w3), tok),
                  pl.BlockSpec((8, w3), lambda i: (jnp.maximum(i * (tm // 8) - 1, 0), 0)),
                  pl.BlockSpec((tm, LANES), tok),
                  pl.BlockSpec((CONV_K, w3), lambda i: (0, 0)),
                  pl.BlockSpec((1, LANES), lambda i: (0, 0)), pl.BlockSpec((1, LANES), lambda i: (0, 0))],
        out_specs=out_specs, out_shape=outs,
        scratch_shapes=[pltpu.VMEM((tm, w3), F32), pltpu.VMEM((tm, LANES), F32), pltpu.VMEM((tm, LANES), F32)],
        compiler_params=_cparams("parallel"), name="gdn_local")(qkv_pre, qkv_pre, small, conv_w, alog_row, dtb_row)


def _gdn_scan_kernel(qe_ref, w_ref, u_ref, kt_ref, ol_ref, dl_ref, z_ref, nw_ref, o_ref, s_sc, o_sc, *, tm, batch):
    c_sz = GDN_CHUNK
    n_chunks = tm // c_sz
    gw = 4 * GDN_D

    @pl.when(pl.program_id(0) == 0)
    def _():
        s_sc[...] = jnp.zeros(s_sc.shape, F32)

    rg = lax.broadcasted_iota(jnp.int32, (gw, gw), 0) // GDN_D
    cg = lax.broadcasted_iota(jnp.int32, (gw, gw), 1) // GDN_D
    bd_mask = rg == cg

    def chunk_body(c, carry):
        r0 = pl.multiple_of(c * c_sz, c_sz)
        rows = pl.ds(r0, c_sz)
        for b in range(batch):
            dl = dl_ref[b, pl.ds(c, 1), :]
            for gi in range(GDN_HEADS // 4):
                cols = slice(gi * gw, (gi + 1) * gw)
                s = s_sc[b, gi]
                lhs = jnp.concatenate([qe_ref[b, rows, cols], w_ref[b, rows, cols]], axis=0).astype(BF16)
                x = _dot(lhs, s.astype(BF16))
                o_sc[b, rows, cols] = x[:c_sz] + ol_ref[b, rows, cols]
                v_new = u_ref[b, rows, cols] - x[c_sz:]
                upd = _dot_tn(kt_ref[b, rows, cols].astype(BF16), v_new.astype(BF16))
                s_sc[b, gi] = s * dl[:, cols] + jnp.where(bd_mask, upd, 0.0)
        return carry

    lax.fori_loop(0, n_chunks, chunk_body, 0)

    w5 = GDN_WIDTH
    rh = lax.broadcasted_iota(jnp.int32, (w5, w5), 0) // GDN_D
    ch = lax.broadcasted_iota(jnp.int32, (w5, w5), 1) // GDN_D
    ones_bd = (rh == ch).astype(BF16)
    for b in range(batch):
        o = o_sc[b]
        sq = o * o
        hi = sq.astype(BF16)
        mid = (sq - hi.astype(F32)).astype(BF16)
        lo = (sq - hi.astype(F32) - mid.astype(F32)).astype(BF16)
        ms = (_dot(hi, ones_bd) + _dot(mid, ones_bd) + _dot(lo, ones_bd)) * (1.0 / GDN_D)
        o_ref[b] = (o * lax.rsqrt(ms + RMS_EPS) * nw_ref[...] * _silu(z_ref[b])).astype(o_ref.dtype)


def _gdn_scan(qe, w, u, kt, ol, dl, z, nw_row, batch, seq, out_dtype, tm=512):
    r3 = lambda a: a.reshape(batch, seq, GDN_WIDTH)
    blk = pl.BlockSpec((batch, tm, GDN_WIDTH), lambda i: (0, i, 0))
    dl3 = dl.reshape(batch, seq // GDN_CHUNK, GDN_WIDTH)
    out = pl.pallas_call(
        functools.partial(_gdn_scan_kernel, tm=tm, batch=batch), grid=(seq // tm,),
        in_specs=[blk] * 5 + [pl.BlockSpec((batch, tm // GDN_CHUNK, GDN_WIDTH), lambda i: (0, i, 0)), blk,
                              pl.BlockSpec((1, GDN_WIDTH), lambda i: (0, 0))],
        out_specs=blk, out_shape=jax.ShapeDtypeStruct((batch, seq, GDN_WIDTH), out_dtype),
        scratch_shapes=[pltpu.VMEM((batch, GDN_HEADS // 4, 4 * GDN_D, 4 * GDN_D), F32),
                        pltpu.VMEM((batch, tm, GDN_WIDTH), F32)],
        compiler_params=_cparams("arbitrary"), name="gdn_scan")(r3(qe), r3(w), r3(u), r3(kt), r3(ol), dl3, r3(z), nw_row)
    return out.reshape(batch * seq, GDN_WIDTH)


def _rope_kernel(pos_ref, inv_ref, cos_ref, sin_ref):
    ang = pos_ref[...].astype(F32) * inv_ref[...]
    lane = lax.broadcasted_iota(jnp.int32, ang.shape, 1)
    cos_ref[...] = jnp.where(lane < MLA_NOPE, 1.0, jnp.where(lane < MLA_NOPE + MLA_ROPE, jnp.cos(ang), 0.0))
    sin_ref[...] = jnp.where(lane >= MLA_NOPE + MLA_ROPE, jnp.sin(ang), 0.0)


def _rope_tables(pos_col, inv_row, tm=512):
    t = pos_col.shape[0]
    blk = pl.BlockSpec((tm, LANES), lambda i: (i, 0))
    return pl.pallas_call(
        _rope_kernel, grid=(t // tm,),
        in_specs=[pl.BlockSpec((tm, 1), lambda i: (i, 0)), pl.BlockSpec((1, LANES), lambda i: (0, 0))],
        out_specs=[blk, blk], out_shape=[jax.ShapeDtypeStruct((t, LANES), F32)] * 2,
        compiler_params=_cparams("parallel"), name="rope_tables")(pos_col, inv_row)


def _rotary(blocks, cos, sin, n_heads):
    width = n_heads * LANES
    cos_t = jnp.concatenate([cos] * n_heads, axis=1) if n_heads > 1 else cos
    sin_t = jnp.concatenate([sin] * n_heads, axis=1) if n_heads > 1 else sin
    return blocks * cos_t + pltpu.roll(blocks * sin_t, width - MLA_ROPE, 1)


def _mla_q_kernel(cq_ref, g_ref, w_ref, cos_ref, sin_ref, o_ref):
    y = _rms_norm(cq_ref[...], g_ref[...])
    q = _dot(y.astype(BF16), w_ref[...])
    o_ref[...] = _rotary(q, cos_ref[...], sin_ref[...], MLA_HEADS).astype(o_ref.dtype)


def _mla_q(c_q, gain_row, w_q, cos, sin, tm=256):
    t = c_q.shape[0]
    width = MLA_HEADS * LANES
    tok = lambda i: (i, 0)
    fixed = lambda i: (0, 0)
    return pl.pallas_call(
        _mla_q_kernel, grid=(t // tm,),
        in_specs=[pl.BlockSpec((tm, MLA_Q_LORA), tok), pl.BlockSpec((1, MLA_Q_LORA), fixed),
                  pl.BlockSpec(w_q.shape, fixed), pl.BlockSpec((tm, LANES), tok), pl.BlockSpec((tm, LANES), tok)],
        out_specs=pl.BlockSpec((tm, width), tok), out_shape=jax.ShapeDtypeStruct((t, width), BF16),
        compiler_params=_cparams("parallel"), name="mla_q")(c_q, gain_row, w_q, cos, sin)


def _mla_kv_kernel(ckv_ref, g_ref, kr_ref, wk_ref, wv_ref, cos_ref, sin_ref, k_ref, v_ref):
    y = _rms_norm(ckv_ref[...], g_ref[...]).astype(BF16)
    kr = _rotary(kr_ref[...], cos_ref[...], sin_ref[...], 1)
    lane = lax.broadcasted_iota(jnp.int32, kr.shape, 1)
    kr = jnp.where((lane >= MLA_NOPE) & (lane < MLA_NOPE + MLA_ROPE), kr, 0.0)
    k = _dot(y, wk_ref[...]) + jnp.concatenate([kr] * MLA_HEADS, axis=1)
    k_ref[...] = k.astype(k_ref.dtype)
    v_ref[...] = _dot(y, wv_ref[...]).astype(v_ref.dtype)


def _mla_kv(c_kv, gain_row, kr_blk, w_k, w_v, cos, sin, tm=256):
    t = c_kv.shape[0]
    tok = lambda i: (i, 0)
    fixed = lambda i: (0, 0)
    return pl.pallas_call(
        _mla_kv_kernel, grid=(t // tm,),
        in_specs=[pl.BlockSpec((tm, MLA_KV_LORA), tok), pl.BlockSpec((1, MLA_KV_LORA), fixed),
                  pl.BlockSpec((tm, LANES), tok), pl.BlockSpec(w_k.shape, fixed), pl.BlockSpec(w_v.shape, fixed),
                  pl.BlockSpec((tm, LANES), tok), pl.BlockSpec((tm, LANES), tok)],
        out_specs=[pl.BlockSpec((tm, w_k.shape[1]), tok), pl.BlockSpec((tm, w_v.shape[1]), tok)],
        out_shape=[jax.ShapeDtypeStruct((t, w_k.shape[1]), BF16), jax.ShapeDtypeStruct((t, w_v.shape[1]), BF16)],
        compiler_params=_cparams("parallel"), name="mla_kv")(c_kv, gain_row, kr_blk, w_k, w_v, cos, sin)


def _router_kernel(x_ref, wt_ref, bias_ref, idx_ref, gate_ref, *, tm):
    logits = _dot_nt(wt_ref[...], x_ref[...], HIGHEST)
    scores = jax.nn.sigmoid(logits)
    biased = scores + bias_ref[...]
    epg = EXPERTS_PER_GROUP
    iota = lax.broadcasted_iota(jnp.int32, (epg, tm), 0)
    best = None
    for gi in range(N_GROUPS):
        blk = biased[gi * epg:(gi + 1) * epg, :]
        raw = scores[gi * epg:(gi + 1) * epg, :]
        m1 = jnp.max(blk, axis=0, keepdims=True)
        i1 = jnp.min(jnp.where(blk == m1, iota, epg), axis=0, keepdims=True)
        blk2 = jnp.where(iota == i1, -jnp.inf, blk)
        m2 = jnp.max(blk2, axis=0, keepdims=True)
        i2 = jnp.min(jnp.where(blk2 == m2, iota, epg), axis=0, keepdims=True)
        s1 = jnp.sum(jnp.where(iota == i1, raw, 0.0), axis=0, keepdims=True)
        s2 = jnp.sum(jnp.where(iota == i2, raw, 0.0), axis=0, keepdims=True)
        cand = (m1 + m2, gi * epg + i1, gi * epg + i2, s1, s2)
        if best is None:
            best = cand
        else:
            better = cand[0] > best[0]
            best = tuple(jnp.where(better, c, b) for c, b in zip(cand, best))
    _, e1, e2, s1, s2 = best
    denom = s1 + s2
    idx_ref[...] = jnp.concatenate([e1, e2], axis=0)
    gate_ref[...] = jnp.concatenate([s1 / denom, s2 / denom], axis=0)


def _router(xt, router_wt, bias_col, tm=512):
    t, d = xt.shape
    return pl.pallas_call(
        functools.partial(_router_kernel, tm=tm), grid=(t // tm,),
        in_specs=[pl.BlockSpec((tm, d), lambda i: (i, 0)), pl.BlockSpec((N_EXPERTS, d), lambda i: (0, 0)),
                  pl.BlockSpec((N_EXPERTS, 1), lambda i: (0, 0))],
        out_specs=[pl.BlockSpec((TOP_K, tm), lambda i: (0, i)), pl.BlockSpec((TOP_K, tm), lambda i: (0, i))],
        out_shape=[jax.ShapeDtypeStruct((TOP_K, t), jnp.int32), jax.ShapeDtypeStruct((TOP_K, t), F32)],
        compiler_params=_cparams("parallel"), name="router")(xt, router_wt, bias_col)


def _row_copy(src_hbm, row, buf, slot, r, sem):
    return pltpu.make_async_copy(src_hbm.at[pl.ds(row, 1)], buf.at[slot, pl.ds(r, 1)], sem.at[slot])


def _gather_start(idx_ref, base, n_rows, src_hbm, buf, slot, sem):
    def body(r, carry):
        _row_copy(src_hbm, idx_ref[base + r], buf, slot, r, sem).start()
        return carry
    lax.fori_loop(0, n_rows, body, 0, unroll=8)


def _gather_wait(n_rows, src_hbm, buf, slot, sem):
    def body(r, carry):
        _row_copy(src_hbm, 0, buf, slot, r, sem).wait()
        return carry
    lax.fori_loop(0, n_rows, body, 0, unroll=8)


def _moe_ffn_kernel(be_ref, rt_ref, nb_ref, x_hbm, wg_ref, wu_ref, wd_ref, o_ref, rows, sem, *, tm):
    i = pl.program_id(0)
    n_used = nb_ref[0]
    slot = i % 2

    @pl.when(i == 0)
    def _():
        _gather_start(rt_ref, 0, tm, x_hbm, rows, 0, sem)

    @pl.when(i + 1 < n_used)
    def _():
        _gather_start(rt_ref, (i + 1) * tm, tm, x_hbm, rows, 1 - slot, sem)

    @pl.when(i < n_used)
    def _():
        _gather_wait(tm, x_hbm, rows, slot, sem)
        r = rows[slot].astype(BF16)
        hidden = _silu(_dot(r, wg_ref[...])) * _dot(r, wu_ref[...])
        o_ref[...] = _dot(hidden.astype(BF16), wd_ref[...])

    @pl.when(i >= n_used)
    def _():
        o_ref[...] = jnp.zeros(o_ref.shape, o_ref.dtype)


def _moe_ffn(xt, block_expert, row_tok, n_used, w_gate, w_up, w_down, n_blocks, tm):
    t, d = xt.shape
    ff = w_gate.shape[2]
    grid_spec = pltpu.PrefetchScalarGridSpec(
        num_scalar_prefetch=3, grid=(n_blocks,),
        in_specs=[pl.BlockSpec(memory_space=pl.ANY),
                  pl.BlockSpec((None, d, ff), lambda i, be, rt, nb: (be[i], 0, 0)),
                  pl.BlockSpec((None, d, ff), lambda i, be, rt, nb: (be[i], 0, 0)),
                  pl.BlockSpec((None, ff, d), lambda i, be, rt, nb: (be[i], 0, 0))],
        out_specs=pl.BlockSpec((tm, d), lambda i, be, rt, nb: (i, 0)),
        scratch_shapes=[pltpu.VMEM((2, tm, d), F32), pltpu.SemaphoreType.DMA((2,))])
    return pl.pallas_call(
        functools.partial(_moe_ffn_kernel, tm=tm), grid_spec=grid_spec,
        out_shape=jax.ShapeDtypeStruct((n_blocks * tm, d), F32),
        compiler_params=_cparams("arbitrary"), name="moe_ffn")(block_expert, row_tok, n_used, xt, w_gate, w_up, w_down)


def _combine_ln_kernel(pos_ref, rows_hbm, x_ref, gt_ref, g_ref, b_ref, o_ref, buf, sem, *, tm, n_tok, n_steps, alpha):
    i = pl.program_id(0)
    slot = i % 2

    def start(step, sl):
        for kk in range(TOP_K):
            def body(r, carry):
                _row_copy(rows_hbm, pos_ref[kk * n_tok + step * tm + r], buf, sl, kk * tm + r, sem).start()
                return carry
            lax.fori_loop(0, tm, body, 0, unroll=8)

    @pl.when(i == 0)
    def _():
        start(0, 0)

    @pl.when(i + 1 < n_steps)
    def _():
        start(i + 1, 1 - slot)

    _gather_wait(TOP_K * tm, rows_hbm, buf, slot, sem)
    gt = gt_ref[...]
    y = gt[:, 0:1] * buf[slot, 0:tm, :] + gt[:, 1:2] * buf[slot, tm:2 * tm, :]
    r = alpha * x_ref[...] + y
    o_ref[...] = _layer_norm(r, g_ref[...], b_ref[...])


def _combine_ln(out_rows, pos_flat, xt, gates_tk, gain, bias, alpha, tm=128):
    t, d = xt.shape
    n_steps = t // tm
    grid_spec = pltpu.PrefetchScalarGridSpec(
        num_scalar_prefetch=1, grid=(n_steps,),
        in_specs=[pl.BlockSpec(memory_space=pl.ANY),
                  pl.BlockSpec((tm, d), lambda i, pos: (i, 0)),
                  pl.BlockSpec((tm, TOP_K), lambda i, pos: (i, 0)),
                  pl.BlockSpec((1, d), lambda i, pos: (0, 0)), pl.BlockSpec((1, d), lambda i, pos: (0, 0))],
        out_specs=pl.BlockSpec((tm, d), lambda i, pos: (i, 0)),
        scratch_shapes=[pltpu.VMEM((2, TOP_K * tm, d), F32), pltpu.SemaphoreType.DMA((2,))])
    return pl.pallas_call(
        functools.partial(_combine_ln_kernel, tm=tm, n_tok=t, n_steps=n_steps, alpha=alpha), grid_spec=grid_spec,
        out_shape=jax.ShapeDtypeStruct((t, d), F32),
        compiler_params=_cparams("arbitrary"), name="combine_ln")(
            pos_flat, out_rows, xt, gates_tk, gain.reshape(1, d), bias.reshape(1, d))


MOE_TM = 256


def _moe_layer(xt, router_wt, bias_col, w_gate, w_up, w_down, gain, bias, alpha):
    t, d = xt.shape
    n_pairs = t * TOP_K
    n_blocks = n_pairs // MOE_TM + N_EXPERTS
    idx, gates = _router(xt, router_wt, bias_col)
    pair_expert = idx.T.reshape(n_pairs)
    onehot = (pair_expert[:, None] == jnp.arange(N_EXPERTS, dtype=jnp.int32)[None, :]).astype(jnp.int32)
    csum = jnp.cumsum(onehot, axis=0)
    counts = csum[-1]
    rank = jnp.take_along_axis(csum, pair_expert[:, None], axis=1)[:, 0] - 1
    padded = (counts + MOE_TM - 1) // MOE_TM * MOE_TM
    padded_end = jnp.cumsum(padded)
    dest = (padded_end - padded)[pair_expert] + rank
    row_tok = jnp.zeros((n_blocks * MOE_TM,), jnp.int32).at[dest].set(jnp.arange(n_pairs, dtype=jnp.int32) // TOP_K)
    block_expert = jnp.minimum(
        jnp.searchsorted(padded_end, jnp.arange(n_blocks, dtype=jnp.int32) * MOE_TM, side='right'),
        N_EXPERTS - 1).astype(jnp.int32)
    n_used = (padded_end[-1:] // MOE_TM).astype(jnp.int32)
    out_rows = _moe_ffn(xt, block_expert, row_tok, n_used, w_gate, w_up, w_down, n_blocks, MOE_TM)
    pos_flat = dest.reshape(t, TOP_K).T.reshape(n_pairs).astype(jnp.int32)
    return _combine_ln(out_rows, pos_flat, xt, gates.T, gain, bias, alpha)


def _delta_fox_layer(xt, batch, seq, w_in, conv_w, a_log, dt_bias, norm_w, f_bias, w_out, gain, bias, alpha):
    o_qkv, o_z, o_beta, o_a, o_fox, o_f = 0, 1536, 2048, 2056, 2064, 3600
    zeros = lambda n: jnp.zeros((w_in.shape[0], n), w_in.dtype)
    w_small = jnp.concatenate([w_in[:, o_f:o_f + 8], w_in[:, o_beta:o_beta + 8], w_in[:, o_a:o_a + 8], zeros(LANES - 24)], axis=1)
    ws = [w_in[:, o_qkv:o_z], w_in[:, o_z:o_beta], w_small, w_in[:, o_fox:o_f]]
    qkv_pre, z, small, fox_qkv = _proj(xt, [w.astype(BF16) for w in ws], [F32, F32, F32, BF16], tm=256)

    def lane_row(v, off):
        return jnp.zeros((1, LANES), F32).at[0, off:off + v.shape[0]].set(v.astype(F32))

    qe, w, u, kt, ol, dl = _gdn_local(qkv_pre, small, conv_w.astype(F32), lane_row(a_log, 16), lane_row(dt_bias, 16), batch, seq)
    o_d = _gdn_scan(qe, w, u, kt, ol, dl, z, jnp.tile(norm_w.astype(F32), GDN_HEADS).reshape(1, GDN_WIDTH), batch, seq, BF16)
    cq, ck = _fox_gate(small, lane_row(f_bias, 0), batch, seq)
    ck = ck.reshape(batch, FOX_HEADS // 2, 2, seq)
    o_f = _attention(fox_qkv, 0, fox_qkv, FOX_WIDTH // LANES, fox_qkv, 2 * FOX_WIDTH // LANES, FOX_HEADS, batch, seq,
                     FOX_D ** -0.5, True, BF16, cq=cq, ck=ck)
    w_out = w_out.astype(BF16)
    return _out_ln([o_d, o_f], [w_out[:GDN_WIDTH], w_out[GDN_WIDTH:]], xt, gain, bias, alpha, tm=256)


def _mla_layer(xt, positions, batch, seq, w_in, q_norm, kv_norm, w_uq, w_ukv, w_out, gain, bias, alpha):
    t = xt.shape[0]
    half = MLA_ROPE // 2
    o_kv, o_kr = MLA_Q_LORA, MLA_Q_LORA + MLA_KV_LORA

    def rope_cols(w):
        return jnp.concatenate([w, -w[:, half:], w[:, :half]], axis=1)

    w_kr = jnp.concatenate([jnp.zeros((w_in.shape[0], MLA_NOPE), w_in.dtype), rope_cols(w_in[:, o_kr:])], axis=1)
    ws = [w_in[:, :o_kv], w_in[:, o_kv:o_kr], w_kr]
    c_q, c_kv, kr_blk = _proj(xt, [w.astype(BF16) for w in ws], [F32, F32, F32], tm=256)

    dq = MLA_NOPE + MLA_ROPE
    wq3 = w_uq.reshape(MLA_Q_LORA, MLA_HEADS, dq)
    wq_blk = jnp.concatenate([wq3, -wq3[:, :, MLA_NOPE + half:], wq3[:, :, MLA_NOPE:MLA_NOPE + half]], axis=2)
    wq_blk = wq_blk.reshape(MLA_Q_LORA, MLA_HEADS * LANES).astype(BF16)
    wkv3 = w_ukv.reshape(MLA_KV_LORA, MLA_HEADS, MLA_NOPE + MLA_V)
    wk_blk = jnp.concatenate([wkv3[:, :, :MLA_NOPE], jnp.zeros((MLA_KV_LORA, MLA_HEADS, LANES - MLA_NOPE), w_ukv.dtype)], axis=2)
    wk_blk = wk_blk.reshape(MLA_KV_LORA, MLA_HEADS * LANES).astype(BF16)
    wv_blk = wkv3[:, :, MLA_NOPE:].reshape(MLA_KV_LORA, MLA_HEADS * MLA_V).astype(BF16)

    inv_freq = ROPE_THETA ** (-jnp.arange(0, MLA_ROPE, 2, dtype=F32) / MLA_ROPE)
    inv_row = jnp.concatenate([jnp.zeros((MLA_NOPE,), F32)] + [inv_freq] * 4).reshape(1, LANES)
    cos, sin = _rope_tables(positions.reshape(t, 1).astype(jnp.int32), inv_row)

    q_blk = _mla_q(c_q, q_norm.astype(F32).reshape(1, -1), wq_blk, cos, sin)
    k_blk, v_blk = _mla_kv(c_kv, kv_norm.astype(F32).reshape(1, -1), kr_blk, wk_blk, wv_blk, cos, sin)
    o = _attention(q_blk, 0, k_blk, 0, v_blk, 0, MLA_HEADS, batch, seq, dq ** -0.5, False, BF16)
    return _out_ln([o], [w_out.astype(BF16)], xt, gain, bias, alpha, tm=256)


def kernel(x, positions, ln_gain, ln_bias, router_w, router_bias, moe_w_gate, moe_w_up, moe_w_down, hy_w_in, hy_conv_w,
           gdn_a_log, gdn_dt_bias, gdn_norm_w, fox_f_bias, hy_w_out, mla_w_in, mla_q_norm, mla_kv_norm, mla_w_uq,
           mla_w_ukv, mla_w_out):
    batch, seq, d = x.shape
    depth = ln_gain.shape[0]
    alpha = (2.0 * depth) ** 0.25
    xt = x.reshape(batch * seq, d)
    router_wt = router_w.astype(F32).T
    bias_col = router_bias.astype(F32).reshape(N_EXPERTS, 1)
    for layer in range(depth):
        j = layer // 2
        if layer % 2 == 0:
            xt = _delta_fox_layer(xt, batch, seq, hy_w_in[j], hy_conv_w[j], gdn_a_log[j], gdn_dt_bias[j], gdn_norm_w[j],
                                  fox_f_bias[j], hy_w_out[j], ln_gain[layer, 0], ln_bias[layer, 0], alpha)
        else:
            xt = _mla_layer(xt, positions, batch, seq, mla_w_in[j], mla_q_norm[j], mla_kv_norm[j], mla_w_uq[j],
                            mla_w_ukv[j], mla_w_out[j], ln_gain[layer, 0], ln_bias[layer, 0], alpha)
        xt = _moe_layer(xt, router_wt, bias_col, moe_w_gate[layer].astype(BF16), moe_w_up[layer].astype(BF16),
                        moe_w_down[layer].astype(BF16), ln_gain[layer, 1], ln_bias[layer, 1], alpha)
    return xt.reshape(batch, seq, d)
```

```python
import functools
import math

import numpy as np
import jax
import jax.numpy as jnp
from jax import lax
from jax.experimental import pallas as pl
from jax.experimental.pallas import tpu as pltpu

F32 = jnp.float32
BF16 = jnp.bfloat16

D_MODEL = 1024
LN_EPS = 1e-5
RMS_EPS = 1e-6
NEG_INF = -1e30
LANES = 128

GDN_HEADS = 8
GDN_D = 64
GDN_CHUNK = 64
CONV_K = 4
GDN_WIDTH = GDN_HEADS * GDN_D
FOX_HEADS = 8
FOX_D = 64
FOX_WIDTH = FOX_HEADS * FOX_D
MLA_HEADS = 16
MLA_Q_LORA = 512
MLA_KV_LORA = 256
MLA_NOPE = 64
MLA_ROPE = 32
MLA_V = 64
ROPE_THETA = 10000.0
N_EXPERTS = 64
N_GROUPS = 8
EXPERTS_PER_GROUP = 8
TOP_K = 2
D_FF_EXPERT = 256

VMEM_LIMIT = 56 * 1024 * 1024
HIGHEST = lax.Precision.HIGHEST
LOG2E = 1.4426950408889634


def _cparams(*sem, flags=None):
    return pltpu.CompilerParams(dimension_semantics=sem, vmem_limit_bytes=VMEM_LIMIT, flags=flags)


def _dot(a, b, precision=None):
    return jnp.dot(a, b, preferred_element_type=F32, precision=precision)


def _dot_nt(a, b, precision=None):
    return lax.dot_general(a, b, (((1,), (1,)), ((), ())), preferred_element_type=F32, precision=precision)


def _dot_tn(a, b, precision=None):
    return lax.dot_general(a, b, (((0,), (0,)), ((), ())), preferred_element_type=F32, precision=precision)


def _silu(x):
    return x * jax.nn.sigmoid(x)


def _layer_norm(r, gain, bias):
    mu = jnp.mean(r, axis=-1, keepdims=True)
    d = r - mu
    var = jnp.mean(d * d, axis=-1, keepdims=True)
    return d * lax.rsqrt(var + LN_EPS) * gain + bias


def _rms_norm(x, gain):
    return x * lax.rsqrt(jnp.mean(x * x, axis=-1, keepdims=True) + RMS_EPS) * gain


def _proj_kernel(x_ref, *refs, out_scales):
    n = len(refs) // 2
    x = x_ref[...].astype(BF16)
    for w_ref, o_ref, scale in zip(refs[:n], refs[n:], out_scales):
        y = _dot(x, w_ref[...])
        o_ref[...] = (y if scale == 1.0 else y * scale).astype(o_ref.dtype)


def _proj(x, ws, out_dtypes, tm, out_scales=None):
    m, k = x.shape
    out_scales = tuple(out_scales) if out_scales is not None else (1.0,) * len(ws)
    in_specs = [pl.BlockSpec((tm, k), lambda i: (i, 0))]
    in_specs += [pl.BlockSpec(w.shape, lambda i: (0, 0)) for w in ws]
    out_specs = [pl.BlockSpec((tm, w.shape[1]), lambda i: (i, 0)) for w in ws]
    out_shape = [jax.ShapeDtypeStruct((m, w.shape[1]), dt) for w, dt in zip(ws, out_dtypes)]
    return pl.pallas_call(
        functools.partial(_proj_kernel, out_scales=out_scales), grid=(m // tm,), in_specs=in_specs, out_specs=out_specs, out_shape=out_shape,
        compiler_params=_cparams("parallel"), name="proj")(x, *ws)


def _out_ln_kernel(*refs, n_in, alpha):
    a_refs = refs[:n_in]
    w_refs = refs[n_in:2 * n_in]
    x_ref, g_ref, b_ref, o_ref = refs[2 * n_in:]
    y = None
    for a_ref, w_ref in zip(a_refs, w_refs):
        t = _dot(a_ref[...].astype(BF16), w_ref[...])
        y = t if y is None else y + t
    r = alpha * x_ref[...] + y
    o_ref[...] = _layer_norm(r, g_ref[...], b_ref[...])


def _out_ln(acts, ws, x, gain, bias, alpha, tm):
    m, d = x.shape
    n_in = len(acts)
    in_specs = [pl.BlockSpec((tm, a.shape[1]), lambda i: (i, 0)) for a in acts]
    in_specs += [pl.BlockSpec(w.shape, lambda i: (0, 0)) for w in ws]
    in_specs += [pl.BlockSpec((tm, d), lambda i: (i, 0)),
                 pl.BlockSpec((1, d), lambda i: (0, 0)), pl.BlockSpec((1, d), lambda i: (0, 0))]
    return pl.pallas_call(
        functools.partial(_out_ln_kernel, n_in=n_in, alpha=alpha),
        grid=(m // tm,), in_specs=in_specs, out_specs=pl.BlockSpec((tm, d), lambda i: (i, 0)),
        out_shape=jax.ShapeDtypeStruct((m, d), F32),
        compiler_params=_cparams("parallel"), name="out_ln")(*acts, *ws, x, gain.reshape(1, d), bias.reshape(1, d))


def _attn_kernel(*refs, tq, tkc, packed, has_bias):
    refs = list(refs)
    q_sc = refs.pop() if packed else None
    if has_bias:
        q_ref, k_ref, v_ref, cq_ref, ck_ref, o_ref, m_sc, acc_sc, cq_sc = refs
    else:
        q_ref, k_ref, v_ref, o_ref, m_sc, acc_sc = refs
    p_id = pl.program_id(1)
    i = pl.program_id(2)
    n_below = (i * tq) // tkc
    n_all = ((i + 1) * tq + tkc - 1) // tkc

    m_sc[...] = jnp.full(m_sc.shape, NEG_INF, F32)
    acc_sc[...] = jnp.zeros(acc_sc.shape, F32)
    lane_q = lax.broadcasted_iota(jnp.int32, (tq, LANES), 1)
    if has_bias:
        cq = cq_ref[...]
        for hh in range(2):
            col = jnp.sum(jnp.where(lane_q == 2 * p_id + hh, cq, 0.0), axis=1, keepdims=True)
            cq_sc[hh] = jnp.broadcast_to(col, (tq, LANES))
    lane_v = lax.broadcasted_iota(jnp.int32, (tkc, LANES), 1)
    if packed:
        q2 = q_ref[...]
        for hh in range(2):
            q_sc[hh] = jnp.where((lane_q >= 64 * hh) & (lane_q < 64 * (hh + 1)), q2, jnp.zeros_like(q2))
    n_blk = tkc // LANES

    def chunk(c, masked):
        k0 = pl.multiple_of(c * tkc, tkc)
        v = v_ref[pl.ds(k0, tkc), :]
        ones = jnp.ones_like(v)
        v_aug = (jnp.where(lane_v < 64, v, ones), jnp.where(lane_v < 64, ones, v))
        if masked:
            row = i * tq + lax.broadcasted_iota(jnp.int32, (tq, LANES), 0)
            col = k0 + lax.broadcasted_iota(jnp.int32, (tq, LANES), 1)
        for hh in range(2):
            if packed:
                q = q_sc[hh]
                k = k_ref[pl.ds(k0, tkc), :]
            else:
                q = q_ref[:, hh * LANES:(hh + 1) * LANES]
                k = k_ref[pl.ds(k0, tkc), hh * LANES:(hh + 1) * LANES]
            s = _dot_nt(q, k)
            if has_bias:
                ck_row = ck_ref[0, 0, hh:hh + 1, pl.ds(k0, tkc)]
            blocks = []
            for bi in range(n_blk):
                sb = s[:, bi * LANES:(bi + 1) * LANES]
                if has_bias:
                    sb = sb - ck_row[:, bi * LANES:(bi + 1) * LANES]
                if masked:
                    sb = jnp.where(col + bi * LANES <= row, sb, NEG_INF)
                blocks.append(sb)
            m_cur = functools.reduce(jnp.maximum, blocks)
            m_cur = jnp.broadcast_to(jnp.max(m_cur, axis=1, keepdims=True), (tq, LANES))
            if has_bias:
                m_cur = m_cur + cq_sc[hh]
            m_prev = m_sc[hh]
            m_new = jnp.maximum(m_prev, m_cur)
            alpha = jnp.exp2(m_prev - m_new)
            shift = m_new - cq_sc[hh] if has_bias else m_new
            p = jnp.concatenate([jnp.exp2(sb - shift).astype(BF16) for sb in blocks], axis=1)
            acc_sc[hh] = alpha * acc_sc[hh] + _dot(p, v_aug[hh])
            m_sc[hh] = m_new

    def below(c, carry):
        chunk(c, False)
        return carry

    def diagonal(c, carry):
        chunk(c, True)
        return carry

    lax.fori_loop(0, n_below, below, 0)
    lax.fori_loop(n_below, n_all, diagonal, 0)

    acc0 = acc_sc[0]
    acc1 = acc_sc[1]
    o0 = acc0 / pltpu.roll(acc0, 64, 1)
    o1 = acc1 / pltpu.roll(acc1, 64, 1)
    o_ref[...] = jnp.where(lane_q < 64, o0, o1).astype(o_ref.dtype)


def _attention(q_arr, q_off, k_arr, k_off, v_arr, v_off, n_heads, batch, seq, packed, out_dtype,
               cq=None, ck=None, tq=1024, tkc=512, flags=None):
    t = batch * seq
    hp = n_heads // 2
    qk_w = LANES if packed else 2 * LANES
    nq = seq // tq
    has_bias = cq is not None
    in_specs = [pl.BlockSpec((tq, qk_w), lambda b, p, i: (b * nq + i, q_off + p)),
                pl.BlockSpec((seq, qk_w), lambda b, p, i: (b, k_off + p)),
                pl.BlockSpec((seq, LANES), lambda b, p, i: (b, v_off + p))]
    args = [q_arr, k_arr, v_arr]
    scratch = [pltpu.VMEM((2, tq, LANES), F32), pltpu.VMEM((2, tq, LANES), F32)]
    if has_bias:
        in_specs += [pl.BlockSpec((tq, LANES), lambda b, p, i: (b * nq + i, 0)),
                     pl.BlockSpec((1, 1, 2, seq), lambda b, p, i: (b, p, 0, 0))]
        args += [cq, ck]
        scratch += [pltpu.VMEM((2, tq, LANES), F32)]
    if packed:
        scratch += [pltpu.VMEM((2, tq, LANES), q_arr.dtype)]
    return pl.pallas_call(
        functools.partial(_attn_kernel, tq=tq, tkc=tkc, packed=packed, has_bias=has_bias),
        grid=(batch, hp, nq), in_specs=in_specs,
        out_specs=pl.BlockSpec((tq, LANES), lambda b, p, i: (b * nq + i, p)),
        out_shape=jax.ShapeDtypeStruct((t, hp * LANES), out_dtype),
        scratch_shapes=scratch,
        compiler_params=_cparams("parallel", "parallel", "arbitrary", flags=flags), name="attention")(*args)


def _fox_gate_kernel(s_ref, fb_ref, cq_ref, ck_ref, carry, *, tm):
    @pl.when(pl.program_id(1) == 0)
    def _():
        carry[...] = jnp.zeros(carry.shape, F32)

    logf = jax.nn.log_sigmoid(s_ref[...] + fb_ref[...])
    row = lax.broadcasted_iota(jnp.int32, (tm, tm), 0)
    col = lax.broadcasted_iota(jnp.int32, (tm, tm), 1)
    tri = (col <= row).astype(F32)
    c = _dot(tri, logf, HIGHEST) + carry[0:1, :]
    carry[...] = jnp.broadcast_to(c[tm - 1:tm, :], carry.shape)
    c2 = c * LOG2E
    cq_ref[...] = c2
    ck_ref[0] = c2.T[0:8, :]


def _fox_gate(small, f_bias_row, batch, seq, tm=256):
    t = batch * seq
    n = seq // tm
    return pl.pallas_call(
        functools.partial(_fox_gate_kernel, tm=tm), grid=(batch, n),
        in_specs=[pl.BlockSpec((tm, LANES), lambda b, i: (b * n + i, 0)), pl.BlockSpec((1, LANES), lambda b, i: (0, 0))],
        out_specs=[pl.BlockSpec((tm, LANES), lambda b, i: (b * n + i, 0)), pl.BlockSpec((1, 8, tm), lambda b, i: (b, 0, i))],
        out_shape=[jax.ShapeDtypeStruct((t, LANES), F32), jax.ShapeDtypeStruct((batch, 8, seq), F32)],
        scratch_shapes=[pltpu.VMEM((8, LANES), F32)],
        compiler_params=_cparams("parallel", "arbitrary"), name="fox_gate")(small, f_bias_row)


def _gdn_local_kernel(x_ref, halo_ref, s_ref, cw_ref, alog_ref, dtb_ref,
                      qe_ref, w_ref, u_ref, kt_ref, ol_ref, dl_ref,
                      y_sc, beta_sc, ld_sc, *, tm, tiles_per_batch):
    c_sz = GDN_CHUNK
    n_chunks = tm // c_sz
    first = (pl.program_id(0) % tiles_per_batch) == 0
    halo = jnp.where(first, 0.0, halo_ref[...])
    ext = jnp.concatenate([halo, x_ref[...]], axis=0)
    acc = None
    for jj in range(CONV_K):
        term = cw_ref[jj:jj + 1, :] * ext[8 - (CONV_K - 1) + jj: 8 - (CONV_K - 1) + jj + tm, :]
        acc = term if acc is None else acc + term
    y_sc[...] = _silu(acc)
    sm = s_ref[...]
    beta_sc[...] = jax.nn.sigmoid(sm)
    ld_sc[...] = -jnp.exp(alog_ref[...]) * jax.nn.softplus(sm + dtb_ref[...])

    r64 = lax.broadcasted_iota(jnp.int32, (c_sz, c_sz), 0)
    c64 = lax.broadcasted_iota(jnp.int32, (c_sz, c_sz), 1)
    tri64 = (c64 <= r64).astype(F32)
    lane64 = lax.broadcasted_iota(jnp.int32, (c_sz, LANES), 1) < 64
    rowp = lax.broadcasted_iota(jnp.int32, (LANES, LANES), 0)
    colp = lax.broadcasted_iota(jnp.int32, (LANES, LANES), 1)
    row_h1 = rowp >= 64
    same_head = row_h1 == (colp >= 64)
    ti = rowp % 64
    tj = colp % 64
    incl = same_head & (tj <= ti)
    strict = same_head & (tj < ti)
    eye = (rowp == colp).astype(F32)
    lane_h1_full = lax.broadcasted_iota(jnp.int32, (LANES, LANES), 1) >= 64
    scale = GDN_D ** -0.5

    def sel(a0, a1):
        return jnp.where(lane64, a0, a1)

    def chunk_body(c, carry):
        r0 = pl.multiple_of(c * c_sz, c_sz)
        rows = pl.ds(r0, c_sz)
        bl = beta_sc[rows, :]
        g = _dot(tri64, ld_sc[rows, :], HIGHEST)
        g_t = jnp.concatenate([g, g], axis=0).T
        dl_row = []
        for p in range(GDN_HEADS // 2):
            h0, h1 = 2 * p, 2 * p + 1
            q2 = y_sc[rows, p * LANES:(p + 1) * LANES]
            k2 = y_sc[rows, GDN_WIDTH + p * LANES:GDN_WIDTH + (p + 1) * LANES]
            v2 = y_sc[rows, 2 * GDN_WIDTH + p * LANES:2 * GDN_WIDTH + (p + 1) * LANES]

            def l2n(x2):
                sq = x2 * x2
                s0 = jnp.sum(jnp.where(lane64, sq, 0.0), axis=1, keepdims=True)
                s1 = jnp.sum(jnp.where(lane64, 0.0, sq), axis=1, keepdims=True)
                return x2 * sel(lax.rsqrt(s0 + RMS_EPS), lax.rsqrt(s1 + RMS_EPS))

            kn2 = l2n(k2)
            qs2 = l2n(q2) * scale
            bcol2 = sel(bl[:, 8 + h0:9 + h0], bl[:, 8 + h1:9 + h1])
            g0c = g[:, 16 + h0:17 + h0]
            g1c = g[:, 16 + h1:17 + h1]
            gcol2 = sel(g0c, g1c)
            eg2 = jnp.exp(gcol2)
            kb2 = kn2 * bcol2
            vb2 = v2 * bcol2
            kbg2 = kb2 * eg2
            qdec2 = qs2 * eg2
            gl0 = g[c_sz - 1:c_sz, 16 + h0:17 + h0]
            gl1 = g[c_sz - 1:c_sz, 16 + h1:17 + h1]
            glast2 = sel(gl0, gl1)
            ktail2 = kn2 * jnp.exp(glast2 - gcol2)
            dl_row.append(jnp.exp(glast2[0:1, :]))

            gcol_p = jnp.concatenate([jnp.broadcast_to(g0c, (c_sz, LANES)), jnp.broadcast_to(g1c, (c_sz, LANES))], axis=0)
            grow_p = jnp.where(row_h1, g_t[16 + h1:17 + h1, :], g_t[16 + h0:17 + h0, :])
            gdiff = gcol_p - grow_p
            decay = jnp.where(incl, jnp.exp(jnp.where(incl, gdiff, 0.0)), 0.0)

            kk = jnp.concatenate([kn2, kn2], axis=0).astype(BF16)
            kbm = jnp.concatenate([jnp.where(lane64, kb2, 0.0), jnp.where(lane64, 0.0, kb2)], axis=0).astype(BF16)
            qm = jnp.concatenate([jnp.where(lane64, qs2, 0.0), jnp.where(lane64, 0.0, qs2)], axis=0).astype(BF16)
            lower = jnp.where(strict, _dot_nt(kbm, kk) * decay, 0.0)
            attn = jnp.where(incl, _dot_nt(qm, kk) * decay, 0.0)

            mpow = -lower
            tinv = eye + mpow
            for _ in range(5):
                mpow = _dot(mpow, mpow, HIGHEST)
                tinv = tinv + _dot(tinv, mpow, HIGHEST)

            rhs = jnp.concatenate([jnp.concatenate([vb2, vb2], axis=0), jnp.concatenate([kbg2, kbg2], axis=0)], axis=1)
            uw = _dot(tinv.astype(BF16), rhs.astype(BF16))
            olq = _dot(attn.astype(BF16), uw.astype(BF16))
            u2 = sel(uw[:c_sz, :LANES], uw[c_sz:, :LANES])
            w2 = sel(uw[:c_sz, LANES:], uw[c_sz:, LANES:])
            ol2 = sel(olq[:c_sz, :LANES], olq[c_sz:, :LANES])
            aw2 = sel(olq[:c_sz, LANES:], olq[c_sz:, LANES:])
            cols = slice(p * LANES, (p + 1) * LANES)
            qe_ref[rows, cols] = qdec2 - aw2
            w_ref[rows, cols] = w2
            u_ref[rows, cols] = u2
            kt_ref[rows, cols] = ktail2
            ol_ref[rows, cols] = ol2
        dl_ref[pl.ds(c, 1), :] = jnp.concatenate(dl_row, axis=1)
        return carry

    lax.fori_loop(0, n_chunks, chunk_body, 0)


def _gdn_local(qkv_pre, small, conv_w, alog_row, dtb_row, batch, seq, tm=512):
    t = batch * seq
    w3 = 3 * GDN_WIDTH
    tiles_per_batch = seq // tm
    tok = lambda i: (i, 0)
    outs = [jax.ShapeDtypeStruct((t, GDN_WIDTH), F32)] * 5 + [jax.ShapeDtypeStruct((t // GDN_CHUNK, GDN_WIDTH), F32)]
    out_specs = [pl.BlockSpec((tm, GDN_WIDTH), tok)] * 5 + [pl.BlockSpec((tm // GDN_CHUNK, GDN_WIDTH), tok)]
    return pl.pallas_call(
        functools.partial(_gdn_local_kernel, tm=tm, tiles_per_batch=tiles_per_batch),
        grid=(t // tm,),
        in_specs=[pl.BlockSpec((tm, w3), tok),
                  pl.BlockSpec((8, w3), lambda i: (jnp.maximum(i * (tm // 8) - 1, 0), 0)),
                  pl.BlockSpec((tm, LANES), tok),
                  pl.BlockSpec((CONV_K, w3), lambda i: (0, 0)),
                  pl.BlockSpec((1, LANES), lambda i: (0, 0)), pl.BlockSpec((1, LANES), lambda i: (0, 0))],
        out_specs=out_specs, out_shape=outs,
        scratch_shapes=[pltpu.VMEM((tm, w3), F32), pltpu.VMEM((tm, LANES), F32), pltpu.VMEM((tm, LANES), F32)],
        compiler_params=_cparams("parallel"), name="gdn_local")(qkv_pre, qkv_pre, small, conv_w, alog_row, dtb_row)


def _gdn_scan_kernel(qe_ref, w_ref, u_ref, kt_ref, ol_ref, dl_ref, z_ref, nw_ref, o_ref, s_sc, o_sc, *, tm, batch):
    c_sz = GDN_CHUNK
    n_chunks = tm // c_sz
    gw = 4 * GDN_D

    @pl.when(pl.program_id(0) == 0)
    def _():
        s_sc[...] = jnp.zeros(s_sc.shape, F32)

    rg = lax.broadcasted_iota(jnp.int32, (gw, gw), 0) // GDN_D
    cg = lax.broadcasted_iota(jnp.int32, (gw, gw), 1) // GDN_D
    bd_mask = rg == cg

    def chunk_body(c, carry):
        r0 = pl.multiple_of(c * c_sz, c_sz)
        rows = pl.ds(r0, c_sz)
        for b in range(batch):
            dl = dl_ref[b, pl.ds(c, 1), :]
            for gi in range(GDN_HEADS // 4):
                cols = slice(gi * gw, (gi + 1) * gw)
                s = s_sc[b, gi]
                lhs = jnp.concatenate([qe_ref[b, rows, cols], w_ref[b, rows, cols]], axis=0).astype(BF16)
                x = _dot(lhs, s.astype(BF16))
                o_sc[b, rows, cols] = x[:c_sz] + ol_ref[b, rows, cols]
                v_new = u_ref[b, rows, cols] - x[c_sz:]
                upd = _dot_tn(kt_ref[b, rows, cols].astype(BF16), v_new.astype(BF16))
                s_sc[b, gi] = s * dl[:, cols] + jnp.where(bd_mask, upd, 0.0)
        return carry

    lax.fori_loop(0, n_chunks, chunk_body, 0)

    w5 = GDN_WIDTH
    rh = lax.broadcasted_iota(jnp.int32, (w5, w5), 0) // GDN_D
    ch = lax.broadcasted_iota(jnp.int32, (w5, w5), 1) // GDN_D
    ones_bd = (rh == ch).astype(BF16)
    for b in range(batch):
        o = o_sc[b]
        sq = o * o
        hi = sq.astype(BF16)
        mid = (sq - hi.astype(F32)).astype(BF16)
        lo = (sq - hi.astype(F32) - mid.astype(F32)).astype(BF16)
        ms = (_dot(hi, ones_bd) + _dot(mid, ones_bd) + _dot(lo, ones_bd)) * (1.0 / GDN_D)
        o_ref[b] = (o * lax.rsqrt(ms + RMS_EPS) * nw_ref[...] * _silu(z_ref[b])).astype(o_ref.dtype)


def _gdn_scan(qe, w, u, kt, ol, dl, z, nw_row, batch, seq, out_dtype, tm=512):
    r3 = lambda a: a.reshape(batch, seq, GDN_WIDTH)
    blk = pl.BlockSpec((batch, tm, GDN_WIDTH), lambda i: (0, i, 0))
    dl3 = dl.reshape(batch, seq // GDN_CHUNK, GDN_WIDTH)
    out = pl.pallas_call(
        functools.partial(_gdn_scan_kernel, tm=tm, batch=batch), grid=(seq // tm,),
        in_specs=[blk] * 5 + [pl.BlockSpec((batch, tm // GDN_CHUNK, GDN_WIDTH), lambda i: (0, i, 0)), blk,
                              pl.BlockSpec((1, GDN_WIDTH), lambda i: (0, 0))],
        out_specs=blk, out_shape=jax.ShapeDtypeStruct((batch, seq, GDN_WIDTH), out_dtype),
        scratch_shapes=[pltpu.VMEM((batch, GDN_HEADS // 4, 4 * GDN_D, 4 * GDN_D), F32),
                        pltpu.VMEM((batch, tm, GDN_WIDTH), F32)],
        compiler_params=_cparams("arbitrary"), name="gdn_scan")(r3(qe), r3(w), r3(u), r3(kt), r3(ol), dl3, r3(z), nw_row)
    return out.reshape(batch * seq, GDN_WIDTH)


def _rope_kernel(pos_ref, inv_ref, cos_ref, sin_ref):
    ang = pos_ref[...].astype(F32) * inv_ref[...]
    lane = lax.broadcasted_iota(jnp.int32, ang.shape, 1)
    cos_ref[...] = jnp.where(lane < MLA_NOPE, 1.0, jnp.where(lane < MLA_NOPE + MLA_ROPE, jnp.cos(ang), 0.0))
    sin_ref[...] = jnp.where(lane >= MLA_NOPE + MLA_ROPE, jnp.sin(ang), 0.0)


def _rope_tables(pos_col, inv_row, tm=512):
    t = pos_col.shape[0]
    blk = pl.BlockSpec((tm, LANES), lambda i: (i, 0))
    return pl.pallas_call(
        _rope_kernel, grid=(t // tm,),
        in_specs=[pl.BlockSpec((tm, 1), lambda i: (i, 0)), pl.BlockSpec((1, LANES), lambda i: (0, 0))],
        out_specs=[blk, blk], out_shape=[jax.ShapeDtypeStruct((t, LANES), F32)] * 2,
        compiler_params=_cparams("parallel"), name="rope_tables")(pos_col, inv_row)


def _rotary(blocks, cos, sin, n_heads):
    width = n_heads * LANES
    cos_t = jnp.concatenate([cos] * n_heads, axis=1) if n_heads > 1 else cos
    sin_t = jnp.concatenate([sin] * n_heads, axis=1) if n_heads > 1 else sin
    return blocks * cos_t + pltpu.roll(blocks * sin_t, width - MLA_ROPE, 1)


def _mla_q_kernel(cq_ref, g_ref, w_ref, cos_ref, sin_ref, o_ref):
    y = _rms_norm(cq_ref[...], g_ref[...])
    q = _dot(y.astype(BF16), w_ref[...])
    q = _rotary(q, cos_ref[...], sin_ref[...], MLA_HEADS)
    o_ref[...] = (q * ((MLA_NOPE + MLA_ROPE) ** -0.5 * LOG2E)).astype(o_ref.dtype)


def _mla_q(c_q, gain_row, w_q, cos, sin, tm=256):
    t = c_q.shape[0]
    width = MLA_HEADS * LANES
    tok = lambda i: (i, 0)
    fixed = lambda i: (0, 0)
    return pl.pallas_call(
        _mla_q_kernel, grid=(t // tm,),
        in_specs=[pl.BlockSpec((tm, MLA_Q_LORA), tok), pl.BlockSpec((1, MLA_Q_LORA), fixed),
                  pl.BlockSpec(w_q.shape, fixed), pl.BlockSpec((tm, LANES), tok), pl.BlockSpec((tm, LANES), tok)],
        out_specs=pl.BlockSpec((tm, width), tok), out_shape=jax.ShapeDtypeStruct((t, width), BF16),
        compiler_params=_cparams("parallel"), name="mla_q")(c_q, gain_row, w_q, cos, sin)


def _mla_kv_kernel(ckv_ref, g_ref, kr_ref, wk_ref, wv_ref, cos_ref, sin_ref, k_ref, v_ref):
    y = _rms_norm(ckv_ref[...], g_ref[...]).astype(BF16)
    kr = _rotary(kr_ref[...], cos_ref[...], sin_ref[...], 1)
    lane = lax.broadcasted_iota(jnp.int32, kr.shape, 1)
    kr = jnp.where((lane >= MLA_NOPE) & (lane < MLA_NOPE + MLA_ROPE), kr, 0.0)
    k = _dot(y, wk_ref[...]) + jnp.concatenate([kr] * MLA_HEADS, axis=1)
    k_ref[...] = k.astype(k_ref.dtype)
    v_ref[...] = _dot(y, wv_ref[...]).astype(v_ref.dtype)


def _mla_kv(c_kv, gain_row, kr_blk, w_k, w_v, cos, sin, tm=256):
    t = c_kv.shape[0]
    tok = lambda i: (i, 0)
    fixed = lambda i: (0, 0)
    return pl.pallas_call(
        _mla_kv_kernel, grid=(t // tm,),
        in_specs=[pl.BlockSpec((tm, MLA_KV_LORA), tok), pl.BlockSpec((1, MLA_KV_LORA), fixed),
                  pl.BlockSpec((tm, LANES), tok), pl.BlockSpec(w_k.shape, fixed), pl.BlockSpec(w_v.shape, fixed),
                  pl.BlockSpec((tm, LANES), tok), pl.BlockSpec((tm, LANES), tok)],
        out_specs=[pl.BlockSpec((tm, w_k.shape[1]), tok), pl.BlockSpec((tm, w_v.shape[1]), tok)],
        out_shape=[jax.ShapeDtypeStruct((t, w_k.shape[1]), BF16), jax.ShapeDtypeStruct((t, w_v.shape[1]), BF16)],
        compiler_params=_cparams("parallel"), name="mla_kv")(c_kv, gain_row, kr_blk, w_k, w_v, cos, sin)


def _router_kernel(x_ref, wt_ref, bias_ref, idx_ref, gate_ref, *, tm):
    logits = _dot_nt(wt_ref[...], x_ref[...], HIGHEST)
    scores = jax.nn.sigmoid(logits)
    biased = scores + bias_ref[...]
    epg = EXPERTS_PER_GROUP
    iota = lax.broadcasted_iota(jnp.int32, (epg, tm), 0)
    best = None
    for gi in range(N_GROUPS):
        blk = biased[gi * epg:(gi + 1) * epg, :]
        raw = scores[gi * epg:(gi + 1) * epg, :]
        m1 = jnp.max(blk, axis=0, keepdims=True)
        i1 = jnp.min(jnp.where(blk == m1, iota, epg), axis=0, keepdims=True)
        blk2 = jnp.where(iota == i1, -jnp.inf, blk)
        m2 = jnp.max(blk2, axis=0, keepdims=True)
        i2 = jnp.min(jnp.where(blk2 == m2, iota, epg), axis=0, keepdims=True)
        s1 = jnp.sum(jnp.where(iota == i1, raw, 0.0), axis=0, keepdims=True)
        s2 = jnp.sum(jnp.where(iota == i2, raw, 0.0), axis=0, keepdims=True)
        cand = (m1 + m2, gi * epg + i1, gi * epg + i2, s1, s2)
        if best is None:
            best = cand
        else:
            better = cand[0] > best[0]
            best = tuple(jnp.where(better, c, b) for c, b in zip(cand, best))
    _, e1, e2, s1, s2 = best
    denom = s1 + s2
    idx_ref[...] = jnp.concatenate([e1, e2], axis=0)
    gate_ref[...] = jnp.concatenate([s1 / denom, s2 / denom], axis=0)


def _router(xt, router_wt, bias_col, tm=512):
    t, d = xt.shape
    return pl.pallas_call(
        functools.partial(_router_kernel, tm=tm), grid=(t // tm,),
        in_specs=[pl.BlockSpec((tm, d), lambda i: (i, 0)), pl.BlockSpec((N_EXPERTS, d), lambda i: (0, 0)),
                  pl.BlockSpec((N_EXPERTS, 1), lambda i: (0, 0))],
        out_specs=[pl.BlockSpec((TOP_K, tm), lambda i: (0, i)), pl.BlockSpec((TOP_K, tm), lambda i: (0, i))],
        out_shape=[jax.ShapeDtypeStruct((TOP_K, t), jnp.int32), jax.ShapeDtypeStruct((TOP_K, t), F32)],
        compiler_params=_cparams("parallel"), name="router")(xt, router_wt, bias_col)


def _row_copy(src_hbm, row, buf, slot, r, sem):
    return pltpu.make_async_copy(src_hbm.at[pl.ds(row, 1)], buf.at[slot, pl.ds(r, 1)], sem.at[slot])


def _gather_start(idx_ref, base, n_rows, src_hbm, buf, slot, sem):
    def body(r, carry):
        _row_copy(src_hbm, idx_ref[base + r], buf, slot, r, sem).start()
        return carry
    lax.fori_loop(0, n_rows, body, 0, unroll=8)


def _gather_wait(n_rows, src_hbm, buf, slot, sem):
    def body(r, carry):
        _row_copy(src_hbm, 0, buf, slot, r, sem).wait()
        return carry
    lax.fori_loop(0, n_rows, body, 0, unroll=8)


def _moe_ffn_kernel(be_ref, rt_ref, nb_ref, x_hbm, wg_ref, wu_ref, wd_ref, o_ref, rows, sem, *, tm):
    i = pl.program_id(0)
    n_used = nb_ref[0]
    slot = i % 2

    @pl.when(i == 0)
    def _():
        _gather_start(rt_ref, 0, tm, x_hbm, rows, 0, sem)

    @pl.when(i + 1 < n_used)
    def _():
        _gather_start(rt_ref, (i + 1) * tm, tm, x_hbm, rows, 1 - slot, sem)

    @pl.when(i < n_used)
    def _():
        _gather_wait(tm, x_hbm, rows, slot, sem)
        r = rows[slot].astype(BF16)
        hidden = _silu(_dot(r, wg_ref[...])) * _dot(r, wu_ref[...])
        o_ref[...] = _dot(hidden.astype(BF16), wd_ref[...])

    @pl.when(i >= n_used)
    def _():
        o_ref[...] = jnp.zeros(o_ref.shape, o_ref.dtype)


def _moe_ffn(xt, block_expert, row_tok, n_used, w_gate, w_up, w_down, n_blocks, tm):
    t, d = xt.shape
    ff = w_gate.shape[2]
    grid_spec = pltpu.PrefetchScalarGridSpec(
        num_scalar_prefetch=3, grid=(n_blocks,),
        in_specs=[pl.BlockSpec(memory_space=pl.ANY),
                  pl.BlockSpec((None, d, ff), lambda i, be, rt, nb: (be[i], 0, 0)),
                  pl.BlockSpec((None, d, ff), lambda i, be, rt, nb: (be[i], 0, 0)),
                  pl.BlockSpec((None, ff, d), lambda i, be, rt, nb: (be[i], 0, 0))],
        out_specs=pl.BlockSpec((tm, d), lambda i, be, rt, nb: (i, 0)),
        scratch_shapes=[pltpu.VMEM((2, tm, d), F32), pltpu.SemaphoreType.DMA((2,))])
    return pl.pallas_call(
        functools.partial(_moe_ffn_kernel, tm=tm), grid_spec=grid_spec,
        out_shape=jax.ShapeDtypeStruct((n_blocks * tm, d), F32),
        compiler_params=_cparams("arbitrary"), name="moe_ffn")(block_expert, row_tok, n_used, xt, w_gate, w_up, w_down)


def _combine_ln_kernel(pos_ref, rows_hbm, x_ref, gt_ref, g_ref, b_ref, o_ref, buf, sem, *, tm, n_tok, n_steps, alpha):
    i = pl.program_id(0)
    slot = i % 2

    def start(step, sl):
        for kk in range(TOP_K):
            def body(r, carry):
                _row_copy(rows_hbm, pos_ref[kk * n_tok + step * tm + r], buf, sl, kk * tm + r, sem).start()
                return carry
            lax.fori_loop(0, tm, body, 0, unroll=8)

    @pl.when(i == 0)
    def _():
        start(0, 0)

    @pl.when(i + 1 < n_steps)
    def _():
        start(i + 1, 1 - slot)

    _gather_wait(TOP_K * tm, rows_hbm, buf, slot, sem)
    gt = gt_ref[...]
    y = gt[:, 0:1] * buf[slot, 0:tm, :] + gt[:, 1:2] * buf[slot, tm:2 * tm, :]
    r = alpha * x_ref[...] + y
    o_ref[...] = _layer_norm(r, g_ref[...], b_ref[...])


def _combine_ln(out_rows, pos_flat, xt, gates_tk, gain, bias, alpha, tm=128):
    t, d = xt.shape
    n_steps = t // tm
    grid_spec = pltpu.PrefetchScalarGridSpec(
        num_scalar_prefetch=1, grid=(n_steps,),
        in_specs=[pl.BlockSpec(memory_space=pl.ANY),
                  pl.BlockSpec((tm, d), lambda i, pos: (i, 0)),
                  pl.BlockSpec((tm, TOP_K), lambda i, pos: (i, 0)),
                  pl.BlockSpec((1, d), lambda i, pos: (0, 0)), pl.BlockSpec((1, d), lambda i, pos: (0, 0))],
        out_specs=pl.BlockSpec((tm, d), lambda i, pos: (i, 0)),
        scratch_shapes=[pltpu.VMEM((2, TOP_K * tm, d), F32), pltpu.SemaphoreType.DMA((2,))])
    return pl.pallas_call(
        functools.partial(_combine_ln_kernel, tm=tm, n_tok=t, n_steps=n_steps, alpha=alpha), grid_spec=grid_spec,
        out_shape=jax.ShapeDtypeStruct((t, d), F32),
        compiler_params=_cparams("arbitrary"), name="combine_ln")(
            pos_flat, out_rows, xt, gates_tk, gain.reshape(1, d), bias.reshape(1, d))


MOE_TM = 256


def _moe_layer(xt, router_wt, bias_col, w_gate, w_up, w_down, gain, bias, alpha):
    t, d = xt.shape
    n_pairs = t * TOP_K
    n_blocks = n_pairs // MOE_TM + N_EXPERTS
    idx, gates = _router(xt, router_wt, bias_col)
    pair_expert = idx.T.reshape(n_pairs)
    onehot = (pair_expert[:, None] == jnp.arange(N_EXPERTS, dtype=jnp.int32)[None, :]).astype(jnp.int32)
    csum = jnp.cumsum(onehot, axis=0)
    counts = csum[-1]
    rank = jnp.take_along_axis(csum, pair_expert[:, None], axis=1)[:, 0] - 1
    padded = (counts + MOE_TM - 1) // MOE_TM * MOE_TM
    padded_end = jnp.cumsum(padded)
    dest = (padded_end - padded)[pair_expert] + rank
    row_tok = jnp.zeros((n_blocks * MOE_TM,), jnp.int32).at[dest].set(jnp.arange(n_pairs, dtype=jnp.int32) // TOP_K)
    block_expert = jnp.minimum(
        jnp.searchsorted(padded_end, jnp.arange(n_blocks, dtype=jnp.int32) * MOE_TM, side='right'),
        N_EXPERTS - 1).astype(jnp.int32)
    n_used = (padded_end[-1:] // MOE_TM).astype(jnp.int32)
    out_rows = _moe_ffn(xt, block_expert, row_tok, n_used, w_gate, w_up, w_down, n_blocks, MOE_TM)
    pos_flat = dest.reshape(t, TOP_K).T.reshape(n_pairs).astype(jnp.int32)
    return _combine_ln(out_rows, pos_flat, xt, gates.T, gain, bias, alpha)


def _delta_fox_layer(xt, batch, seq, w_in, conv_w, a_log, dt_bias, norm_w, f_bias, w_out, gain, bias, alpha):
    o_qkv, o_z, o_beta, o_a, o_fox, o_f = 0, 1536, 2048, 2056, 2064, 3600
    zeros = lambda n: jnp.zeros((w_in.shape[0], n), w_in.dtype)
    w_small = jnp.concatenate([w_in[:, o_f:o_f + 8], w_in[:, o_beta:o_beta + 8], w_in[:, o_a:o_a + 8], zeros(LANES - 24)], axis=1)
    ws = [w_in[:, o_qkv:o_z], w_in[:, o_z:o_beta], w_small, w_in[:, o_fox:o_fox + FOX_WIDTH], w_in[:, o_fox + FOX_WIDTH:o_f]]
    qkv_pre, z, small, fox_q, fox_kv = _proj(
        xt, [w.astype(BF16) for w in ws], [F32, F32, F32, BF16, BF16], tm=256,
        out_scales=(1.0, 1.0, 1.0, FOX_D ** -0.5 * LOG2E, 1.0))

    def lane_row(v, off):
        return jnp.zeros((1, LANES), F32).at[0, off:off + v.shape[0]].set(v.astype(F32))

    qe, w, u, kt, ol, dl = _gdn_local(qkv_pre, small, conv_w.astype(F32), lane_row(a_log, 16), lane_row(dt_bias, 16), batch, seq)
    o_d = _gdn_scan(qe, w, u, kt, ol, dl, z, jnp.tile(norm_w.astype(F32), GDN_HEADS).reshape(1, GDN_WIDTH), batch, seq, BF16)
    cq, ck = _fox_gate(small, lane_row(f_bias, 0), batch, seq)
    ck = ck.reshape(batch, FOX_HEADS // 2, 2, seq)
    o_f = _attention(fox_q, 0, fox_kv, 0, fox_kv, FOX_WIDTH // LANES, FOX_HEADS, batch, seq, True, BF16, cq=cq, ck=ck)
    w_out = w_out.astype(BF16)
    return _out_ln([o_d, o_f], [w_out[:GDN_WIDTH], w_out[GDN_WIDTH:]], xt, gain, bias, alpha, tm=256)


def _mla_layer(xt, positions, batch, seq, w_in, q_norm, kv_norm, w_uq, w_ukv, w_out, gain, bias, alpha):
    t = xt.shape[0]
    half = MLA_ROPE // 2
    o_kv, o_kr = MLA_Q_LORA, MLA_Q_LORA + MLA_KV_LORA

    def rope_cols(w):
        return jnp.concatenate([w, -w[:, half:], w[:, :half]], axis=1)

    w_kr = jnp.concatenate([jnp.zeros((w_in.shape[0], MLA_NOPE), w_in.dtype), rope_cols(w_in[:, o_kr:])], axis=1)
    ws = [w_in[:, :o_kv], w_in[:, o_kv:o_kr], w_kr]
    c_q, c_kv, kr_blk = _proj(xt, [w.astype(BF16) for w in ws], [F32, F32, F32], tm=256)

    dq = MLA_NOPE + MLA_ROPE
    wq3 = w_uq.reshape(MLA_Q_LORA, MLA_HEADS, dq)
    wq_blk = jnp.concatenate([wq3, -wq3[:, :, MLA_NOPE + half:], wq3[:, :, MLA_NOPE:MLA_NOPE + half]], axis=2)
    wq_blk = wq_blk.reshape(MLA_Q_LORA, MLA_HEADS * LANES).astype(BF16)
    wkv3 = w_ukv.reshape(MLA_KV_LORA, MLA_HEADS, MLA_NOPE + MLA_V)
    wk_blk = jnp.concatenate([wkv3[:, :, :MLA_NOPE], jnp.zeros((MLA_KV_LORA, MLA_HEADS, LANES - MLA_NOPE), w_ukv.dtype)], axis=2)
    wk_blk = wk_blk.reshape(MLA_KV_LORA, MLA_HEADS * LANES).astype(BF16)
    wv_blk = wkv3[:, :, MLA_NOPE:].reshape(MLA_KV_LORA, MLA_HEADS * MLA_V).astype(BF16)

    inv_freq = ROPE_THETA ** (-jnp.arange(0, MLA_ROPE, 2, dtype=F32) / MLA_ROPE)
    inv_row = jnp.concatenate([jnp.zeros((MLA_NOPE,), F32)] + [inv_freq] * 4).reshape(1, LANES)
    cos, sin = _rope_tables(positions.reshape(t, 1).astype(jnp.int32), inv_row)

    q_blk = _mla_q(c_q, q_norm.astype(F32).reshape(1, -1), wq_blk, cos, sin)
    k_blk, v_blk = _mla_kv(c_kv, kv_norm.astype(F32).reshape(1, -1), kr_blk, wk_blk, wv_blk, cos, sin)
    o = _attention(q_blk, 0, k_blk, 0, v_blk, 0, MLA_HEADS, batch, seq, False, BF16)
    return _out_ln([o], [w_out.astype(BF16)], xt, gain, bias, alpha, tm=256)


def kernel(x, positions, ln_gain, ln_bias, router_w, router_bias, moe_w_gate, moe_w_up, moe_w_down, hy_w_in, hy_conv_w,
           gdn_a_log, gdn_dt_bias, gdn_norm_w, fox_f_bias, hy_w_out, mla_w_in, mla_q_norm, mla_kv_norm, mla_w_uq,
           mla_w_ukv, mla_w_out):
    batch, seq, d = x.shape
    depth = ln_gain.shape[0]
    alpha = (2.0 * depth) ** 0.25
    xt = x.reshape(batch * seq, d)
    router_wt = router_w.astype(F32).T
    bias_col = router_bias.astype(F32).reshape(N_EXPERTS, 1)
    for layer in range(depth):
        j = layer // 2
        if layer % 2 == 0:
            xt = _delta_fox_layer(xt, batch, seq, hy_w_in[j], hy_conv_w[j], gdn_a_log[j], gdn_dt_bias[j], gdn_norm_w[j],
                                  fox_f_bias[j], hy_w_out[j], ln_gain[layer, 0], ln_bias[layer, 0], alpha)
        else:
            xt = _mla_layer(xt, positions, batch, seq, mla_w_in[j], mla_q_norm[j], mla_kv_norm[j], mla_w_uq[j],
                            mla_w_ukv[j], mla_w_out[j], ln_gain[layer, 0], ln_bias[layer, 0], alpha)
        xt = _moe_layer(xt, router_wt, bias_col, moe_w_gate[layer].astype(BF16), moe_w_up[layer].astype(BF16),
                        moe_w_down[layer].astype(BF16), ln_gain[layer, 1], ln_bias[layer, 1], alpha)
    return xt.reshape(batch, seq, d)
```

```python
import functools
import math

import numpy as np
import jax
import jax.numpy as jnp
from jax import lax
from jax.experimental import pallas as pl
from jax.experimental.pallas import tpu as pltpu

F32 = jnp.float32
BF16 = jnp.bfloat16

D_MODEL = 1024
LN_EPS = 1e-5
RMS_EPS = 1e-6
NEG_INF = -1e30
LANES = 128

GDN_HEADS = 8
GDN_D = 64
GDN_CHUNK = 64
CONV_K = 4
GDN_WIDTH = GDN_HEADS * GDN_D
FOX_HEADS = 8
FOX_D = 64
FOX_WIDTH = FOX_HEADS * FOX_D
MLA_HEADS = 16
MLA_Q_LORA = 512
MLA_KV_LORA = 256
MLA_NOPE = 64
MLA_ROPE = 32
MLA_V = 64
ROPE_THETA = 10000.0
N_EXPERTS = 64
N_GROUPS = 8
EXPERTS_PER_GROUP = 8
TOP_K = 2
D_FF_EXPERT = 256

VMEM_LIMIT = 56 * 1024 * 1024
HIGHEST = lax.Precision.HIGHEST
LOG2E = 1.4426950408889634


def _cparams(*sem, flags=None):
    return pltpu.CompilerParams(dimension_semantics=sem, vmem_limit_bytes=VMEM_LIMIT, flags=flags)


def _dot(a, b, precision=None):
    return jnp.dot(a, b, preferred_element_type=F32, precision=precision)


def _dot_nt(a, b, precision=None):
    return lax.dot_general(a, b, (((1,), (1,)), ((), ())), preferred_element_type=F32, precision=precision)


def _dot_tn(a, b, precision=None):
    return lax.dot_general(a, b, (((0,), (0,)), ((), ())), preferred_element_type=F32, precision=precision)


def _split_bf16(x):
    hi = x.astype(BF16)
    return hi, (x - hi.astype(F32)).astype(BF16)


def _dot3(a, b):
    a_hi, a_lo = _split_bf16(a)
    b_hi, b_lo = _split_bf16(b)
    return (_dot(jnp.concatenate([a_hi, a_lo], axis=1), jnp.concatenate([b_hi, b_hi], axis=0))
            + _dot(a_hi, b_lo))


def _silu(x):
    return x * jax.nn.sigmoid(x)


def _layer_norm(r, gain, bias):
    mu = jnp.mean(r, axis=-1, keepdims=True)
    d = r - mu
    var = jnp.mean(d * d, axis=-1, keepdims=True)
    return d * lax.rsqrt(var + LN_EPS) * gain + bias


def _rms_norm(x, gain):
    return x * lax.rsqrt(jnp.mean(x * x, axis=-1, keepdims=True) + RMS_EPS) * gain


def _proj_kernel(x_ref, *refs, out_scales):
    n = len(refs) // 2
    x = x_ref[...].astype(BF16)
    for w_ref, o_ref, scale in zip(refs[:n], refs[n:], out_scales):
        y = _dot(x, w_ref[...])
        o_ref[...] = (y if scale == 1.0 else y * scale).astype(o_ref.dtype)


def _proj(x, ws, out_dtypes, tm, out_scales=None):
    m, k = x.shape
    out_scales = tuple(out_scales) if out_scales is not None else (1.0,) * len(ws)
    in_specs = [pl.BlockSpec((tm, k), lambda i: (i, 0))]
    in_specs += [pl.BlockSpec(w.shape, lambda i: (0, 0)) for w in ws]
    out_specs = [pl.BlockSpec((tm, w.shape[1]), lambda i: (i, 0)) for w in ws]
    out_shape = [jax.ShapeDtypeStruct((m, w.shape[1]), dt) for w, dt in zip(ws, out_dtypes)]
    return pl.pallas_call(
        functools.partial(_proj_kernel, out_scales=out_scales), grid=(m // tm,), in_specs=in_specs, out_specs=out_specs, out_shape=out_shape,
        compiler_params=_cparams("parallel"), name="proj")(x, *ws)


def _out_ln_kernel(*refs, n_in, alpha):
    a_refs = refs[:n_in]
    w_refs = refs[n_in:2 * n_in]
    x_ref, g_ref, b_ref, o_ref = refs[2 * n_in:]
    y = None
    for a_ref, w_ref in zip(a_refs, w_refs):
        t = _dot(a_ref[...].astype(BF16), w_ref[...])
        y = t if y is None else y + t
    r = alpha * x_ref[...] + y
    o_ref[...] = _layer_norm(r, g_ref[...], b_ref[...])


def _out_ln(acts, ws, x, gain, bias, alpha, tm):
    m, d = x.shape
    n_in = len(acts)
    in_specs = [pl.BlockSpec((tm, a.shape[1]), lambda i: (i, 0)) for a in acts]
    in_specs += [pl.BlockSpec(w.shape, lambda i: (0, 0)) for w in ws]
    in_specs += [pl.BlockSpec((tm, d), lambda i: (i, 0)),
                 pl.BlockSpec((1, d), lambda i: (0, 0)), pl.BlockSpec((1, d), lambda i: (0, 0))]
    return pl.pallas_call(
        functools.partial(_out_ln_kernel, n_in=n_in, alpha=alpha),
        grid=(m // tm,), in_specs=in_specs, out_specs=pl.BlockSpec((tm, d), lambda i: (i, 0)),
        out_shape=jax.ShapeDtypeStruct((m, d), F32),
        compiler_params=_cparams("parallel"), name="out_ln")(*acts, *ws, x, gain.reshape(1, d), bias.reshape(1, d))


def _attn_kernel(*refs, tq, tkc, packed, has_bias):
    refs = list(refs)
    q_sc = refs.pop() if packed else None
    if has_bias:
        q_ref, k_ref, v_ref, cq_ref, ck_ref, o_ref, m_sc, acc_sc, cq_sc = refs
    else:
        q_ref, k_ref, v_ref, o_ref, m_sc, acc_sc = refs
    p_id = pl.program_id(1)
    i = pl.program_id(2)
    n_below = (i * tq) // tkc
    n_all = ((i + 1) * tq + tkc - 1) // tkc

    m_sc[...] = jnp.full(m_sc.shape, NEG_INF, F32)
    acc_sc[...] = jnp.zeros(acc_sc.shape, F32)
    lane_q = lax.broadcasted_iota(jnp.int32, (tq, LANES), 1)
    if has_bias:
        cq = cq_ref[...]
        for hh in range(2):
            col = jnp.sum(jnp.where(lane_q == 2 * p_id + hh, cq, 0.0), axis=1, keepdims=True)
            cq_sc[hh] = jnp.broadcast_to(col, (tq, LANES))
    lane_v = lax.broadcasted_iota(jnp.int32, (tkc, LANES), 1)
    if packed:
        q2 = q_ref[...]
        for hh in range(2):
            q_sc[hh] = jnp.where((lane_q >= 64 * hh) & (lane_q < 64 * (hh + 1)), q2, jnp.zeros_like(q2))
    n_blk = tkc // LANES

    def chunk(c, masked):
        k0 = pl.multiple_of(c * tkc, tkc)
        v = v_ref[pl.ds(k0, tkc), :]
        ones = jnp.ones_like(v)
        v_aug = (jnp.where(lane_v < 64, v, ones), jnp.where(lane_v < 64, ones, v))
        if masked:
            row = i * tq + lax.broadcasted_iota(jnp.int32, (tq, LANES), 0)
            col = k0 + lax.broadcasted_iota(jnp.int32, (tq, LANES), 1)
        def qk(hh):
            if packed:
                return _dot_nt(q_sc[hh], k_ref[pl.ds(k0, tkc), :])
            return _dot_nt(q_ref[:, hh * LANES:(hh + 1) * LANES], k_ref[pl.ds(k0, tkc), hh * LANES:(hh + 1) * LANES])

        logits = None if has_bias else [qk(0), qk(1)]
        for hh in range(2):
            s = qk(hh) if logits is None else logits[hh]
            if has_bias:
                ck_row = ck_ref[0, 0, hh:hh + 1, pl.ds(k0, tkc)]
            blocks = []
            for bi in range(n_blk):
                sb = s[:, bi * LANES:(bi + 1) * LANES]
                if has_bias:
                    sb = sb - ck_row[:, bi * LANES:(bi + 1) * LANES]
                if masked:
                    sb = jnp.where(col + bi * LANES <= row, sb, NEG_INF)
                blocks.append(sb)
            m_cur = functools.reduce(jnp.maximum, blocks)
            m_cur = jnp.broadcast_to(jnp.max(m_cur, axis=1, keepdims=True), (tq, LANES))
            if has_bias:
                m_cur = m_cur + cq_sc[hh]
            m_prev = m_sc[hh]
            m_new = jnp.maximum(m_prev, m_cur)
            alpha = jnp.exp2(m_prev - m_new)
            shift = m_new - cq_sc[hh] if has_bias else m_new
            p = jnp.concatenate([jnp.exp2(sb - shift).astype(BF16) for sb in blocks], axis=1)
            acc_sc[hh] = alpha * acc_sc[hh] + _dot(p, v_aug[hh])
            m_sc[hh] = m_new

    def below(c, carry):
        chunk(c, False)
        return carry

    def diagonal(c, carry):
        chunk(c, True)
        return carry

    lax.fori_loop(0, n_below, below, 0)
    lax.fori_loop(n_below, n_all, diagonal, 0)

    acc0 = acc_sc[0]
    acc1 = acc_sc[1]
    o0 = acc0 / pltpu.roll(acc0, 64, 1)
    o1 = acc1 / pltpu.roll(acc1, 64, 1)
    o_ref[...] = jnp.where(lane_q < 64, o0, o1).astype(o_ref.dtype)


def _attention(q_arr, q_off, k_arr, k_off, v_arr, v_off, n_heads, batch, seq, packed, out_dtype,
               cq=None, ck=None, tq=1024, tkc=512, flags=None):
    t = batch * seq
    hp = n_heads // 2
    qk_w = LANES if packed else 2 * LANES
    nq = seq // tq
    has_bias = cq is not None
    in_specs = [pl.BlockSpec((tq, qk_w), lambda b, p, i: (b * nq + i, q_off + p)),
                pl.BlockSpec((seq, qk_w), lambda b, p, i: (b, k_off + p)),
                pl.BlockSpec((seq, LANES), lambda b, p, i: (b, v_off + p))]
    args = [q_arr, k_arr, v_arr]
    scratch = [pltpu.VMEM((2, tq, LANES), F32), pltpu.VMEM((2, tq, LANES), F32)]
    if has_bias:
        in_specs += [pl.BlockSpec((tq, LANES), lambda b, p, i: (b * nq + i, 0)),
                     pl.BlockSpec((1, 1, 2, seq), lambda b, p, i: (b, p, 0, 0))]
        args += [cq, ck]
        scratch += [pltpu.VMEM((2, tq, LANES), F32)]
    if packed:
        scratch += [pltpu.VMEM((2, tq, LANES), q_arr.dtype)]
    return pl.pallas_call(
        functools.partial(_attn_kernel, tq=tq, tkc=tkc, packed=packed, has_bias=has_bias),
        grid=(batch, hp, nq), in_specs=in_specs,
        out_specs=pl.BlockSpec((tq, LANES), lambda b, p, i: (b * nq + i, p)),
        out_shape=jax.ShapeDtypeStruct((t, hp * LANES), out_dtype),
        scratch_shapes=scratch,
        compiler_params=_cparams("parallel", "parallel", "arbitrary", flags=flags), name="attention")(*args)


def _fox_gate_kernel(s_ref, fb_ref, cq_ref, ck_ref, carry, *, tm):
    @pl.when(pl.program_id(1) == 0)
    def _():
        carry[...] = jnp.zeros(carry.shape, F32)

    logf = jax.nn.log_sigmoid(s_ref[...] + fb_ref[...])
    row = lax.broadcasted_iota(jnp.int32, (tm, tm), 0)
    col = lax.broadcasted_iota(jnp.int32, (tm, tm), 1)
    tri = (col <= row).astype(F32)
    c = _dot(tri, logf, HIGHEST) + carry[0:1, :]
    carry[...] = jnp.broadcast_to(c[tm - 1:tm, :], carry.shape)
    c2 = c * LOG2E
    cq_ref[...] = c2
    ck_ref[0] = c2.T[0:8, :]


def _fox_gate(small, f_bias_row, batch, seq, tm=256):
    t = batch * seq
    n = seq // tm
    return pl.pallas_call(
        functools.partial(_fox_gate_kernel, tm=tm), grid=(batch, n),
        in_specs=[pl.BlockSpec((tm, LANES), lambda b, i: (b * n + i, 0)), pl.BlockSpec((1, LANES), lambda b, i: (0, 0))],
        out_specs=[pl.BlockSpec((tm, LANES), lambda b, i: (b * n + i, 0)), pl.BlockSpec((1, 8, tm), lambda b, i: (b, 0, i))],
        out_shape=[jax.ShapeDtypeStruct((t, LANES), F32), jax.ShapeDtypeStruct((batch, 8, seq), F32)],
        scratch_shapes=[pltpu.VMEM((8, LANES), F32)],
        compiler_params=_cparams("parallel", "arbitrary"), name="fox_gate")(small, f_bias_row)


def _gdn_local_kernel(x_ref, halo_ref, s_ref, cw_ref, alog_ref, dtb_ref,
                      qe_ref, w_ref, u_ref, kt_ref, ol_ref, dl_ref,
                      y_sc, beta_sc, ld_sc, *, tm, tiles_per_batch):
    c_sz = GDN_CHUNK
    n_chunks = tm // c_sz
    first = (pl.program_id(0) % tiles_per_batch) == 0
    halo = jnp.where(first, 0.0, halo_ref[...])
    ext = jnp.concatenate([halo, x_ref[...]], axis=0)
    acc = None
    for jj in range(CONV_K):
        term = cw_ref[jj:jj + 1, :] * ext[8 - (CONV_K - 1) + jj: 8 - (CONV_K - 1) + jj + tm, :]
        acc = term if acc is None else acc + term
    y_sc[...] = _silu(acc)
    sm = s_ref[...]
    beta_sc[...] = jax.nn.sigmoid(sm)
    ld_sc[...] = -jnp.exp(alog_ref[...]) * jax.nn.softplus(sm + dtb_ref[...])

    r64 = lax.broadcasted_iota(jnp.int32, (c_sz, c_sz), 0)
    c64 = lax.broadcasted_iota(jnp.int32, (c_sz, c_sz), 1)
    tri64 = (c64 <= r64).astype(F32)
    lane64 = lax.broadcasted_iota(jnp.int32, (c_sz, LANES), 1) < 64
    rowp = lax.broadcasted_iota(jnp.int32, (LANES, LANES), 0)
    colp = lax.broadcasted_iota(jnp.int32, (LANES, LANES), 1)
    row_h1 = rowp >= 64
    same_head = row_h1 == (colp >= 64)
    ti = rowp % 64
    tj = colp % 64
    incl = same_head & (tj <= ti)
    strict = same_head & (tj < ti)
    eye = (rowp == colp).astype(F32)
    lane_h1_full = lax.broadcasted_iota(jnp.int32, (LANES, LANES), 1) >= 64
    scale = GDN_D ** -0.5

    def sel(a0, a1):
        return jnp.where(lane64, a0, a1)

    def chunk_body(c, carry):
        r0 = pl.multiple_of(c * c_sz, c_sz)
        rows = pl.ds(r0, c_sz)
        bl = beta_sc[rows, :]
        g = _dot(tri64, ld_sc[rows, :], HIGHEST)
        g_t = jnp.concatenate([g, g], axis=0).T
        dl_row = []
        n_pairs = GDN_HEADS // 2
        prep = []
        for p in range(n_pairs):
            h0, h1 = 2 * p, 2 * p + 1
            q2 = y_sc[rows, p * LANES:(p + 1) * LANES]
            k2 = y_sc[rows, GDN_WIDTH + p * LANES:GDN_WIDTH + (p + 1) * LANES]
            v2 = y_sc[rows, 2 * GDN_WIDTH + p * LANES:2 * GDN_WIDTH + (p + 1) * LANES]

            def l2n(x2):
                sq = x2 * x2
                s0 = jnp.sum(jnp.where(lane64, sq, 0.0), axis=1, keepdims=True)
                s1 = jnp.sum(jnp.where(lane64, 0.0, sq), axis=1, keepdims=True)
                return x2 * sel(lax.rsqrt(s0 + RMS_EPS), lax.rsqrt(s1 + RMS_EPS))

            kn2 = l2n(k2)
            qs2 = l2n(q2) * scale
            bcol2 = sel(bl[:, 8 + h0:9 + h0], bl[:, 8 + h1:9 + h1])
            g0c = g[:, 16 + h0:17 + h0]
            g1c = g[:, 16 + h1:17 + h1]
            gcol2 = sel(g0c, g1c)
            eg2 = jnp.exp(gcol2)
            kb2 = kn2 * bcol2
            vb2 = v2 * bcol2
            kbg2 = kb2 * eg2
            qdec2 = qs2 * eg2
            gl0 = g[c_sz - 1:c_sz, 16 + h0:17 + h0]
            gl1 = g[c_sz - 1:c_sz, 16 + h1:17 + h1]
            glast2 = sel(gl0, gl1)
            ktail2 = kn2 * jnp.exp(glast2 - gcol2)
            dl_row.append(jnp.exp(glast2[0:1, :]))

            gcol_p = jnp.concatenate([jnp.broadcast_to(g0c, (c_sz, LANES)), jnp.broadcast_to(g1c, (c_sz, LANES))], axis=0)
            grow_p = jnp.where(row_h1, g_t[16 + h1:17 + h1, :], g_t[16 + h0:17 + h0, :])
            gdiff = gcol_p - grow_p
            decay = jnp.where(incl, jnp.exp(jnp.where(incl, gdiff, 0.0)), 0.0)

            kk = jnp.concatenate([kn2, kn2], axis=0).astype(BF16)
            kbm = jnp.concatenate([jnp.where(lane64, kb2, 0.0), jnp.where(lane64, 0.0, kb2)], axis=0).astype(BF16)
            qm = jnp.concatenate([jnp.where(lane64, qs2, 0.0), jnp.where(lane64, 0.0, qs2)], axis=0).astype(BF16)
            kq = _dot_nt(jnp.concatenate([kbm, qm], axis=0), kk)
            lower = jnp.where(strict, kq[:LANES] * decay, 0.0)
            attn = jnp.where(incl, kq[LANES:] * decay, 0.0)
            rhs = jnp.concatenate([jnp.concatenate([vb2, vb2], axis=0), jnp.concatenate([kbg2, kbg2], axis=0)], axis=1)
            prep.append((qdec2, ktail2, attn.astype(BF16), rhs.astype(BF16), -lower))

        mpow = [pr[4] for pr in prep]
        tinv = [eye + m for m in mpow]
        mpow = [_dot3(m, m) for m in mpow]
        for level in range(4):
            both = [_dot3(m, jnp.concatenate([m, t], axis=1)) for m, t in zip(mpow, tinv)]
            mpow = [bo[:, :LANES] for bo in both]
            tinv = [t + bo[:, LANES:] for t, bo in zip(tinv, both)]
        tinv = [t + _dot3(m, t) for m, t in zip(mpow, tinv)]
        uws = [_dot(t.astype(BF16), pr[3]) for t, pr in zip(tinv, prep)]
        olqs = [_dot(pr[2], uw.astype(BF16)) for uw, pr in zip(uws, prep)]
        for p in range(n_pairs):
            qdec2, ktail2 = prep[p][0], prep[p][1]
            uw, olq = uws[p], olqs[p]
            u2 = sel(uw[:c_sz, :LANES], uw[c_sz:, :LANES])
            w2 = sel(uw[:c_sz, LANES:], uw[c_sz:, LANES:])
            ol2 = sel(olq[:c_sz, :LANES], olq[c_sz:, :LANES])
            aw2 = sel(olq[:c_sz, LANES:], olq[c_sz:, LANES:])
            cols = slice(p * LANES, (p + 1) * LANES)
            qe_ref[rows, cols] = qdec2 - aw2
            w_ref[rows, cols] = w2
            u_ref[rows, cols] = u2
            kt_ref[rows, cols] = ktail2
            ol_ref[rows, cols] = ol2
        dl_ref[pl.ds(c, 1), :] = jnp.concatenate(dl_row, axis=1)
        return carry

    lax.fori_loop(0, n_chunks, chunk_body, 0)


def _gdn_local(qkv_pre, small, conv_w, alog_row, dtb_row, batch, seq, tm=512):
    t = batch * seq
    w3 = 3 * GDN_WIDTH
    tiles_per_batch = seq // tm
    tok = lambda i: (i, 0)
    outs = [jax.ShapeDtypeStruct((t, GDN_WIDTH), F32)] * 5 + [jax.ShapeDtypeStruct((t // GDN_CHUNK, GDN_WIDTH), F32)]
    out_specs = [pl.BlockSpec((tm, GDN_WIDTH), tok)] * 5 + [pl.BlockSpec((tm // GDN_CHUNK, GDN_WIDTH), tok)]
    return pl.pallas_call(
        functools.partial(_gdn_local_kernel, tm=tm, tiles_per_batch=tiles_per_batch),
        grid=(t // tm,),
        in_specs=[pl.BlockSpec((tm, w3), tok),
                  pl.BlockSpec((8, w3), lambda i: (jnp.maximum(i * (tm // 8) - 1, 0), 0)),
                  pl.BlockSpec((tm, LANES), tok),
                  pl.BlockSpec((CONV_K, w3), lambda i: (0, 0)),
                  pl.BlockSpec((1, LANES), lambda i: (0, 0)), pl.BlockSpec((1, LANES), lambda i: (0, 0))],
        out_specs=out_specs, out_shape=outs,
        scratch_shapes=[pltpu.VMEM((tm, w3), F32), pltpu.VMEM((tm, LANES), F32), pltpu.VMEM((tm, LANES), F32)],
        compiler_params=_cparams("parallel"), name="gdn_local")(qkv_pre, qkv_pre, small, conv_w, alog_row, dtb_row)


def _gdn_scan_kernel(qe_ref, w_ref, u_ref, kt_ref, ol_ref, dl_ref, z_ref, nw_ref, o_ref, s_sc, o_sc, *, tm, batch):
    c_sz = GDN_CHUNK
    n_chunks = tm // c_sz
    gw = 4 * GDN_D

    @pl.when(pl.program_id(0) == 0)
    def _():
        s_sc[...] = jnp.zeros(s_sc.shape, F32)

    rg = lax.broadcasted_iota(jnp.int32, (gw, gw), 0) // GDN_D
    cg = lax.broadcasted_iota(jnp.int32, (gw, gw), 1) // GDN_D
    bd_mask = rg == cg

    def chunk_body(c, carry):
        r0 = pl.multiple_of(c * c_sz, c_sz)
        rows = pl.ds(r0, c_sz)
        for b in range(batch):
            dl = dl_ref[b, pl.ds(c, 1), :]
            for gi in range(GDN_HEADS // 4):
                cols = slice(gi * gw, (gi + 1) * gw)
                s = s_sc[b, gi]
                lhs = jnp.concatenate([qe_ref[b, rows, cols], w_ref[b, rows, cols]], axis=0).astype(BF16)
                x = _dot(lhs, s.astype(BF16))
                o_sc[b, rows, cols] = x[:c_sz] + ol_ref[b, rows, cols]
                v_new = u_ref[b, rows, cols] - x[c_sz:]
                upd = _dot_tn(kt_ref[b, rows, cols].astype(BF16), v_new.astype(BF16))
                s_sc[b, gi] = s * dl[:, cols] + jnp.where(bd_mask, upd, 0.0)
        return carry

    lax.fori_loop(0, n_chunks, chunk_body, 0)

    w5 = GDN_WIDTH
    rh = lax.broadcasted_iota(jnp.int32, (w5, w5), 0) // GDN_D
    ch = lax.broadcasted_iota(jnp.int32, (w5, w5), 1) // GDN_D
    ones_bd = (rh == ch).astype(BF16)
    for b in range(batch):
        o = o_sc[b]
        sq = o * o
        hi = sq.astype(BF16)
        mid = (sq - hi.astype(F32)).astype(BF16)
        lo = (sq - hi.astype(F32) - mid.astype(F32)).astype(BF16)
        ms = (_dot(hi, ones_bd) + _dot(mid, ones_bd) + _dot(lo, ones_bd)) * (1.0 / GDN_D)
        o_ref[b] = (o * lax.rsqrt(ms + RMS_EPS) * nw_ref[...] * _silu(z_ref[b])).astype(o_ref.dtype)


def _gdn_scan(qe, w, u, kt, ol, dl, z, nw_row, batch, seq, out_dtype, tm=512):
    r3 = lambda a: a.reshape(batch, seq, GDN_WIDTH)
    blk = pl.BlockSpec((batch, tm, GDN_WIDTH), lambda i: (0, i, 0))
    dl3 = dl.reshape(batch, seq // GDN_CHUNK, GDN_WIDTH)
    out = pl.pallas_call(
        functools.partial(_gdn_scan_kernel, tm=tm, batch=batch), grid=(seq // tm,),
        in_specs=[blk] * 5 + [pl.BlockSpec((batch, tm // GDN_CHUNK, GDN_WIDTH), lambda i: (0, i, 0)), blk,
                              pl.BlockSpec((1, GDN_WIDTH), lambda i: (0, 0))],
        out_specs=blk, out_shape=jax.ShapeDtypeStruct((batch, seq, GDN_WIDTH), out_dtype),
        scratch_shapes=[pltpu.VMEM((batch, GDN_HEADS // 4, 4 * GDN_D, 4 * GDN_D), F32),
                        pltpu.VMEM((batch, tm, GDN_WIDTH), F32)],
        compiler_params=_cparams("arbitrary"), name="gdn_scan")(r3(qe), r3(w), r3(u), r3(kt), r3(ol), dl3, r3(z), nw_row)
    return out.reshape(batch * seq, GDN_WIDTH)


def _rope_kernel(pos_ref, inv_ref, cos_ref, sin_ref):
    ang = pos_ref[...].astype(F32) * inv_ref[...]
    lane = lax.broadcasted_iota(jnp.int32, ang.shape, 1)
    cos_ref[...] = jnp.where(lane < MLA_NOPE, 1.0, jnp.where(lane < MLA_NOPE + MLA_ROPE, jnp.cos(ang), 0.0))
    sin_ref[...] = jnp.where(lane >= MLA_NOPE + MLA_ROPE, jnp.sin(ang), 0.0)


def _rope_tables(pos_col, inv_row, tm=512):
    t = pos_col.shape[0]
    blk = pl.BlockSpec((tm, LANES), lambda i: (i, 0))
    return pl.pallas_call(
        _rope_kernel, grid=(t // tm,),
        in_specs=[pl.BlockSpec((tm, 1), lambda i: (i, 0)), pl.BlockSpec((1, LANES), lambda i: (0, 0))],
        out_specs=[blk, blk], out_shape=[jax.ShapeDtypeStruct((t, LANES), F32)] * 2,
        compiler_params=_cparams("parallel"), name="rope_tables")(pos_col, inv_row)


def _rotary(blocks, cos, sin, n_heads):
    width = n_heads * LANES
    cos_t = jnp.concatenate([cos] * n_heads, axis=1) if n_heads > 1 else cos
    sin_t = jnp.concatenate([sin] * n_heads, axis=1) if n_heads > 1 else sin
    return blocks * cos_t + pltpu.roll(blocks * sin_t, width - MLA_ROPE, 1)


def _mla_q_kernel(cq_ref, g_ref, w_ref, cos_ref, sin_ref, o_ref):
    y = _rms_norm(cq_ref[...], g_ref[...])
    q = _dot(y.astype(BF16), w_ref[...])
    q = _rotary(q, cos_ref[...], sin_ref[...], MLA_HEADS)
    o_ref[...] = (q * ((MLA_NOPE + MLA_ROPE) ** -0.5 * LOG2E)).astype(o_ref.dtype)


def _mla_q(c_q, gain_row, w_q, cos, sin, tm=256):
    t = c_q.shape[0]
    width = MLA_HEADS * LANES
    tok = lambda i: (i, 0)
    fixed = lambda i: (0, 0)
    return pl.pallas_call(
        _mla_q_kernel, grid=(t // tm,),
        in_specs=[pl.BlockSpec((tm, MLA_Q_LORA), tok), pl.BlockSpec((1, MLA_Q_LORA), fixed),
                  pl.BlockSpec(w_q.shape, fixed), pl.BlockSpec((tm, LANES), tok), pl.BlockSpec((tm, LANES), tok)],
        out_specs=pl.BlockSpec((tm, width), tok), out_shape=jax.ShapeDtypeStruct((t, width), BF16),
        compiler_params=_cparams("parallel"), name="mla_q")(c_q, gain_row, w_q, cos, sin)


def _mla_kv_kernel(ckv_ref, g_ref, kr_ref, wk_ref, wv_ref, cos_ref, sin_ref, k_ref, v_ref):
    y = _rms_norm(ckv_ref[...], g_ref[...]).astype(BF16)
    kr = _rotary(kr_ref[...], cos_ref[...], sin_ref[...], 1)
    lane = lax.broadcasted_iota(jnp.int32, kr.shape, 1)
    kr = jnp.where((lane >= MLA_NOPE) & (lane < MLA_NOPE + MLA_ROPE), kr, 0.0)
    k = _dot(y, wk_ref[...]) + jnp.concatenate([kr] * MLA_HEADS, axis=1)
    k_ref[...] = k.astype(k_ref.dtype)
    v_ref[...] = _dot(y, wv_ref[...]).astype(v_ref.dtype)


def _mla_kv(c_kv, gain_row, kr_blk, w_k, w_v, cos, sin, tm=256):
    t = c_kv.shape[0]
    tok = lambda i: (i, 0)
    fixed = lambda i: (0, 0)
    return pl.pallas_call(
        _mla_kv_kernel, grid=(t // tm,),
        in_specs=[pl.BlockSpec((tm, MLA_KV_LORA), tok), pl.BlockSpec((1, MLA_KV_LORA), fixed),
                  pl.BlockSpec((tm, LANES), tok), pl.BlockSpec(w_k.shape, fixed), pl.BlockSpec(w_v.shape, fixed),
                  pl.BlockSpec((tm, LANES), tok), pl.BlockSpec((tm, LANES), tok)],
        out_specs=[pl.BlockSpec((tm, w_k.shape[1]), tok), pl.BlockSpec((tm, w_v.shape[1]), tok)],
        out_shape=[jax.ShapeDtypeStruct((t, w_k.shape[1]), BF16), jax.ShapeDtypeStruct((t, w_v.shape[1]), BF16)],
        compiler_params=_cparams("parallel"), name="mla_kv")(c_kv, gain_row, kr_blk, w_k, w_v, cos, sin)


def _router_kernel(x_ref, wt_ref, bias_ref, idx_ref, gate_ref, *, tm):
    logits = _dot_nt(wt_ref[...], x_ref[...], HIGHEST)
    scores = jax.nn.sigmoid(logits)
    biased = scores + bias_ref[...]
    epg = EXPERTS_PER_GROUP
    iota = lax.broadcasted_iota(jnp.int32, (epg, tm), 0)
    best = None
    for gi in range(N_GROUPS):
        blk = biased[gi * epg:(gi + 1) * epg, :]
        raw = scores[gi * epg:(gi + 1) * epg, :]
        m1 = jnp.max(blk, axis=0, keepdims=True)
        i1 = jnp.min(jnp.where(blk == m1, iota, epg), axis=0, keepdims=True)
        blk2 = jnp.where(iota == i1, -jnp.inf, blk)
        m2 = jnp.max(blk2, axis=0, keepdims=True)
        i2 = jnp.min(jnp.where(blk2 == m2, iota, epg), axis=0, keepdims=True)
        s1 = jnp.sum(jnp.where(iota == i1, raw, 0.0), axis=0, keepdims=True)
        s2 = jnp.sum(jnp.where(iota == i2, raw, 0.0), axis=0, keepdims=True)
        cand = (m1 + m2, gi * epg + i1, gi * epg + i2, s1, s2)
        if best is None:
            best = cand
        else:
            better = cand[0] > best[0]
            best = tuple(jnp.where(better, c, b) for c, b in zip(cand, best))
    _, e1, e2, s1, s2 = best
    denom = s1 + s2
    idx_ref[...] = jnp.concatenate([e1, e2], axis=0)
    gate_ref[...] = jnp.concatenate([s1 / denom, s2 / denom], axis=0)


def _router(xt, router_wt, bias_col, tm=512):
    t, d = xt.shape
    return pl.pallas_call(
        functools.partial(_router_kernel, tm=tm), grid=(t // tm,),
        in_specs=[pl.BlockSpec((tm, d), lambda i: (i, 0)), pl.BlockSpec((N_EXPERTS, d), lambda i: (0, 0)),
                  pl.BlockSpec((N_EXPERTS, 1), lambda i: (0, 0))],
        out_specs=[pl.BlockSpec((TOP_K, tm), lambda i: (0, i)), pl.BlockSpec((TOP_K, tm), lambda i: (0, i))],
        out_shape=[jax.ShapeDtypeStruct((TOP_K, t), jnp.int32), jax.ShapeDtypeStruct((TOP_K, t), F32)],
        compiler_params=_cparams("parallel"), name="router")(xt, router_wt, bias_col)


def _moe_rank_kernel(idx_ref, start_ref, pos_ref, carry, *, tm):
    @pl.when(pl.program_id(0) == 0)
    def _():
        carry[...] = jnp.zeros(carry.shape, F32)

    idx = idx_ref[...]
    e_iota = lax.broadcasted_iota(jnp.int32, (N_EXPERTS, tm), 0)
    oh = [(e_iota == idx[kk:kk + 1, :]).astype(F32) for kk in range(TOP_K)]
    both = oh[0] + oh[1]
    r_io = lax.broadcasted_iota(jnp.int32, (tm, tm), 0)
    c_io = lax.broadcasted_iota(jnp.int32, (tm, tm), 1)
    before = (r_io < c_io).astype(BF16)
    base = _dot(both.astype(BF16), before) + carry[:, 0:1] + start_ref[...]
    pos = [jnp.sum(o * base, axis=0, keepdims=True) for o in oh]
    pos_ref[...] = jnp.concatenate(pos, axis=0).astype(jnp.int32)
    carry[...] = carry[...] + jnp.sum(both, axis=1, keepdims=True)


def _moe_rank(idx, start_col, tm=512):
    t = idx.shape[1]
    return pl.pallas_call(
        functools.partial(_moe_rank_kernel, tm=tm), grid=(t // tm,),
        in_specs=[pl.BlockSpec((TOP_K, tm), lambda i: (0, i)), pl.BlockSpec((N_EXPERTS, 1), lambda i: (0, 0))],
        out_specs=pl.BlockSpec((TOP_K, tm), lambda i: (0, i)),
        out_shape=jax.ShapeDtypeStruct((TOP_K, t), jnp.int32),
        scratch_shapes=[pltpu.VMEM((N_EXPERTS, LANES), F32)],
        compiler_params=_cparams("arbitrary"), name="moe_rank")(idx, start_col)


def _moe_dispatch_kernel(pos_ref, x_ref, zero_hbm, xs_hbm, sem, *, tm, n_tok):
    del zero_hbm
    base = pl.program_id(0) * tm
    copies = []
    for kk in range(TOP_K):
        for r in range(tm):
            dst = pos_ref[kk * n_tok + base + r]
            cp = pltpu.make_async_copy(x_ref.at[pl.ds(r, 1)], xs_hbm.at[pl.ds(dst, 1)], sem.at[0])
            cp.start()
            copies.append(cp)
    for cp in copies:
        cp.wait()


def _moe_dispatch(xt, pos_flat, n_rows, tm=128):
    t, d = xt.shape
    grid_spec = pltpu.PrefetchScalarGridSpec(
        num_scalar_prefetch=1, grid=(t // tm,),
        in_specs=[pl.BlockSpec((tm, d), lambda i, pos: (i, 0)), pl.BlockSpec(memory_space=pl.ANY)],
        out_specs=pl.BlockSpec(memory_space=pl.ANY),
        scratch_shapes=[pltpu.SemaphoreType.DMA((1,))])
    return pl.pallas_call(
        functools.partial(_moe_dispatch_kernel, tm=tm, n_tok=t), grid_spec=grid_spec,
        out_shape=jax.ShapeDtypeStruct((n_rows, d), xt.dtype),
        input_output_aliases={2: 0},
        compiler_params=_cparams("arbitrary"), name="moe_dispatch")(pos_flat, xt, jnp.zeros((n_rows, d), xt.dtype))


def _moe_ffn_kernel(be_ref, nb_ref, r_ref, wg_ref, wu_ref, wd_ref, o_ref):
    del be_ref
    i = pl.program_id(0)

    @pl.when(i < nb_ref[0])
    def _():
        r = r_ref[...].astype(BF16)
        hidden = _silu(_dot(r, wg_ref[...])) * _dot(r, wu_ref[...])
        o_ref[...] = _dot(hidden.astype(BF16), wd_ref[...])

    @pl.when(i >= nb_ref[0])
    def _():
        o_ref[...] = jnp.zeros(o_ref.shape, o_ref.dtype)


def _moe_ffn(xs, block_expert, n_used, w_gate, w_up, w_down, tm):
    n_rows, d = xs.shape
    ff = w_gate.shape[2]
    grid_spec = pltpu.PrefetchScalarGridSpec(
        num_scalar_prefetch=2, grid=(n_rows // tm,),
        in_specs=[pl.BlockSpec((tm, d), lambda i, be, nb: (i, 0)),
                  pl.BlockSpec((None, d, ff), lambda i, be, nb: (be[i], 0, 0)),
                  pl.BlockSpec((None, d, ff), lambda i, be, nb: (be[i], 0, 0)),
                  pl.BlockSpec((None, ff, d), lambda i, be, nb: (be[i], 0, 0))],
        out_specs=pl.BlockSpec((tm, d), lambda i, be, nb: (i, 0)))
    return pl.pallas_call(
        _moe_ffn_kernel, grid_spec=grid_spec,
        out_shape=jax.ShapeDtypeStruct((n_rows, d), F32),
        compiler_params=_cparams("arbitrary"), name="moe_ffn")(block_expert, n_used, xs, w_gate, w_up, w_down)


def _combine_ln_kernel(pos_ref, rows_hbm, x_ref, gt_ref, g_ref, b_ref, o_ref, buf, sem, *, tm, n_tok, n_steps, alpha):
    i = pl.program_id(0)
    slot = i % 2

    def row_copy(src_row, sl, r):
        return pltpu.make_async_copy(rows_hbm.at[pl.ds(src_row, 1)], buf.at[sl, pl.ds(r, 1)], sem.at[sl])

    def start(step, sl):
        for kk in range(TOP_K):
            for r in range(tm):
                row_copy(pos_ref[kk * n_tok + step * tm + r], sl, kk * tm + r).start()

    @pl.when(i == 0)
    def _():
        start(0, 0)

    @pl.when(i + 1 < n_steps)
    def _():
        start(i + 1, 1 - slot)

    for r in range(TOP_K * tm):
        row_copy(0, slot, r).wait()
    gt = gt_ref[...]
    y = gt[:, 0:1] * buf[slot, 0:tm, :] + gt[:, 1:2] * buf[slot, tm:2 * tm, :]
    r = alpha * x_ref[...] + y
    o_ref[...] = _layer_norm(r, g_ref[...], b_ref[...])


def _combine_ln(out_rows, pos_flat, xt, gates_tk, gain, bias, alpha, tm=128):
    t, d = xt.shape
    n_steps = t // tm
    grid_spec = pltpu.PrefetchScalarGridSpec(
        num_scalar_prefetch=1, grid=(n_steps,),
        in_specs=[pl.BlockSpec(memory_space=pl.ANY),
                  pl.BlockSpec((tm, d), lambda i, pos: (i, 0)),
                  pl.BlockSpec((tm, TOP_K), lambda i, pos: (i, 0)),
                  pl.BlockSpec((1, d), lambda i, pos: (0, 0)), pl.BlockSpec((1, d), lambda i, pos: (0, 0))],
        out_specs=pl.BlockSpec((tm, d), lambda i, pos: (i, 0)),
        scratch_shapes=[pltpu.VMEM((2, TOP_K * tm, d), F32), pltpu.SemaphoreType.DMA((2,))])
    return pl.pallas_call(
        functools.partial(_combine_ln_kernel, tm=tm, n_tok=t, n_steps=n_steps, alpha=alpha), grid_spec=grid_spec,
        out_shape=jax.ShapeDtypeStruct((t, d), F32),
        compiler_params=_cparams("arbitrary"), name="combine_ln")(
            pos_flat, out_rows, xt, gates_tk, gain.reshape(1, d), bias.reshape(1, d))


MOE_TM = 256


def _moe_layer(xt, router_wt, bias_col, w_gate, w_up, w_down, gain, bias, alpha):
    t, d = xt.shape
    n_pairs = t * TOP_K
    n_blocks = n_pairs // MOE_TM + N_EXPERTS
    idx, gates = _router(xt, router_wt, bias_col)
    experts = jnp.arange(N_EXPERTS, dtype=jnp.int32)
    counts = jnp.sum((idx.reshape(n_pairs, 1) == experts[None, :]).astype(jnp.int32), axis=0)
    padded = (counts + MOE_TM - 1) // MOE_TM * MOE_TM
    padded_end = jnp.cumsum(padded)
    block_row0 = jnp.arange(n_blocks, dtype=jnp.int32) * MOE_TM
    block_expert = jnp.minimum(jnp.sum((padded_end[None, :] <= block_row0[:, None]).astype(jnp.int32), axis=1),
                               N_EXPERTS - 1).astype(jnp.int32)
    n_used = (padded_end[-1:] // MOE_TM).astype(jnp.int32)
    pos = _moe_rank(idx, (padded_end - padded).astype(F32).reshape(N_EXPERTS, 1))
    pos_flat = pos.reshape(n_pairs)
    xs = _moe_dispatch(xt, pos_flat, n_blocks * MOE_TM)
    out_rows = _moe_ffn(xs, block_expert, n_used, w_gate, w_up, w_down, MOE_TM)
    return _combine_ln(out_rows, pos_flat, xt, gates.T, gain, bias, alpha)


def _delta_fox_layer(xt, batch, seq, w_in, conv_w, a_log, dt_bias, norm_w, f_bias, w_out, gain, bias, alpha):
    o_qkv, o_z, o_beta, o_a, o_fox, o_f = 0, 1536, 2048, 2056, 2064, 3600
    zeros = lambda n: jnp.zeros((w_in.shape[0], n), w_in.dtype)
    w_small = jnp.concatenate([w_in[:, o_f:o_f + 8], w_in[:, o_beta:o_beta + 8], w_in[:, o_a:o_a + 8], zeros(LANES - 24)], axis=1)
    ws = [w_in[:, o_qkv:o_z], w_in[:, o_z:o_beta], w_small, w_in[:, o_fox:o_fox + FOX_WIDTH], w_in[:, o_fox + FOX_WIDTH:o_f]]
    qkv_pre, z, small, fox_q, fox_kv = _proj(
        xt, [w.astype(BF16) for w in ws], [F32, F32, F32, BF16, BF16], tm=256,
        out_scales=(1.0, 1.0, 1.0, FOX_D ** -0.5 * LOG2E, 1.0))

    def lane_row(v, off):
        return jnp.zeros((1, LANES), F32).at[0, off:off + v.shape[0]].set(v.astype(F32))

    qe, w, u, kt, ol, dl = _gdn_local(qkv_pre, small, conv_w.astype(F32), lane_row(a_log, 16), lane_row(dt_bias, 16), batch, seq)
    o_d = _gdn_scan(qe, w, u, kt, ol, dl, z, jnp.tile(norm_w.astype(F32), GDN_HEADS).reshape(1, GDN_WIDTH), batch, seq, BF16)
    cq, ck = _fox_gate(small, lane_row(f_bias, 0), batch, seq)
    ck = ck.reshape(batch, FOX_HEADS // 2, 2, seq)
    o_f = _attention(fox_q, 0, fox_kv, 0, fox_kv, FOX_WIDTH // LANES, FOX_HEADS, batch, seq, True, BF16, cq=cq, ck=ck)
    w_out = w_out.astype(BF16)
    return _out_ln([o_d, o_f], [w_out[:GDN_WIDTH], w_out[GDN_WIDTH:]], xt, gain, bias, alpha, tm=256)


def _mla_layer(xt, positions, batch, seq, w_in, q_norm, kv_norm, w_uq, w_ukv, w_out, gain, bias, alpha):
    t = xt.shape[0]
    half = MLA_ROPE // 2
    o_kv, o_kr = MLA_Q_LORA, MLA_Q_LORA + MLA_KV_LORA

    def rope_cols(w):
        return jnp.concatenate([w, -w[:, half:], w[:, :half]], axis=1)

    w_kr = jnp.concatenate([jnp.zeros((w_in.shape[0], MLA_NOPE), w_in.dtype), rope_cols(w_in[:, o_kr:])], axis=1)
    ws = [w_in[:, :o_kv], w_in[:, o_kv:o_kr], w_kr]
    c_q, c_kv, kr_blk = _proj(xt, [w.astype(BF16) for w in ws], [F32, F32, F32], tm=256)

    dq = MLA_NOPE + MLA_ROPE
    wq3 = w_uq.reshape(MLA_Q_LORA, MLA_HEADS, dq)
    wq_blk = jnp.concatenate([wq3, -wq3[:, :, MLA_NOPE + half:], wq3[:, :, MLA_NOPE:MLA_NOPE + half]], axis=2)
    wq_blk = wq_blk.reshape(MLA_Q_LORA, MLA_HEADS * LANES).astype(BF16)
    wkv3 = w_ukv.reshape(MLA_KV_LORA, MLA_HEADS, MLA_NOPE + MLA_V)
    wk_blk = jnp.concatenate([wkv3[:, :, :MLA_NOPE], jnp.zeros((MLA_KV_LORA, MLA_HEADS, LANES - MLA_NOPE), w_ukv.dtype)], axis=2)
    wk_blk = wk_blk.reshape(MLA_KV_LORA, MLA_HEADS * LANES).astype(BF16)
    wv_blk = wkv3[:, :, MLA_NOPE:].reshape(MLA_KV_LORA, MLA_HEADS * MLA_V).astype(BF16)

    inv_freq = ROPE_THETA ** (-jnp.arange(0, MLA_ROPE, 2, dtype=F32) / MLA_ROPE)
    inv_row = jnp.concatenate([jnp.zeros((MLA_NOPE,), F32)] + [inv_freq] * 4).reshape(1, LANES)
    cos, sin = _rope_tables(positions.reshape(t, 1).astype(jnp.int32), inv_row)

    q_blk = _mla_q(c_q, q_norm.astype(F32).reshape(1, -1), wq_blk, cos, sin)
    k_blk, v_blk = _mla_kv(c_kv, kv_norm.astype(F32).reshape(1, -1), kr_blk, wk_blk, wv_blk, cos, sin)
    o = _attention(q_blk, 0, k_blk, 0, v_blk, 0, MLA_HEADS, batch, seq, False, BF16)
    return _out_ln([o], [w_out.astype(BF16)], xt, gain, bias, alpha, tm=256)


def kernel(x, positions, ln_gain, ln_bias, router_w, router_bias, moe_w_gate, moe_w_up, moe_w_down, hy_w_in, hy_conv_w,
           gdn_a_log, gdn_dt_bias, gdn_norm_w, fox_f_bias, hy_w_out, mla_w_in, mla_q_norm, mla_kv_norm, mla_w_uq,
           mla_w_ukv, mla_w_out):
    batch, seq, d = x.shape
    depth = ln_gain.shape[0]
    alpha = (2.0 * depth) ** 0.25
    xt = x.reshape(batch * seq, d)
    router_wt = router_w.astype(F32).T
    bias_col = router_bias.astype(F32).reshape(N_EXPERTS, 1)
    for layer in range(depth):
        j = layer // 2
        if layer % 2 == 0:
            xt = _delta_fox_layer(xt, batch, seq, hy_w_in[j], hy_conv_w[j], gdn_a_log[j], gdn_dt_bias[j], gdn_norm_w[j],
                                  fox_f_bias[j], hy_w_out[j], ln_gain[layer, 0], ln_bias[layer, 0], alpha)
        else:
            xt = _mla_layer(xt, positions, batch, seq, mla_w_in[j], mla_q_norm[j], mla_kv_norm[j], mla_w_uq[j],
                            mla_w_ukv[j], mla_w_out[j], ln_gain[layer, 0], ln_bias[layer, 0], alpha)
        xt = _moe_layer(xt, router_wt, bias_col, moe_w_gate[layer].astype(BF16), moe_w_up[layer].astype(BF16),
                        moe_w_down[layer].astype(BF16), ln_gain[layer, 1], ln_bias[layer, 1], alpha)
    return xt.reshape(batch, seq, d)
```

```python
import functools
import math

import numpy as np
import jax
import jax.numpy as jnp
from jax import lax
from jax.experimental import pallas as pl
from jax.experimental.pallas import tpu as pltpu

F32 = jnp.float32
BF16 = jnp.bfloat16

D_MODEL = 1024
LN_EPS = 1e-5
RMS_EPS = 1e-6
NEG_INF = -1e30
LANES = 128

GDN_HEADS = 8
GDN_D = 64
GDN_CHUNK = 64
CONV_K = 4
GDN_WIDTH = GDN_HEADS * GDN_D
FOX_HEADS = 8
FOX_D = 64
FOX_WIDTH = FOX_HEADS * FOX_D
MLA_HEADS = 16
MLA_Q_LORA = 512
MLA_KV_LORA = 256
MLA_NOPE = 64
MLA_ROPE = 32
MLA_V = 64
ROPE_THETA = 10000.0
N_EXPERTS = 64
N_GROUPS = 8
EXPERTS_PER_GROUP = 8
TOP_K = 2
D_FF_EXPERT = 256

VMEM_LIMIT = 56 * 1024 * 1024
HIGHEST = lax.Precision.HIGHEST
LOG2E = 1.4426950408889634


def _cparams(*sem, flags=None):
    return pltpu.CompilerParams(dimension_semantics=sem, vmem_limit_bytes=VMEM_LIMIT, flags=flags)


def _dot(a, b, precision=None):
    return jnp.dot(a, b, preferred_element_type=F32, precision=precision)


def _dot_nt(a, b, precision=None):
    return lax.dot_general(a, b, (((1,), (1,)), ((), ())), preferred_element_type=F32, precision=precision)


def _dot_tn(a, b, precision=None):
    return lax.dot_general(a, b, (((0,), (0,)), ((), ())), preferred_element_type=F32, precision=precision)


def _split_bf16(x):
    hi = x.astype(BF16)
    return hi, (x - hi.astype(F32)).astype(BF16)


def _dot3(a, b):
    a_hi, a_lo = _split_bf16(a)
    b_hi, b_lo = _split_bf16(b)
    return (_dot(jnp.concatenate([a_hi, a_lo], axis=1), jnp.concatenate([b_hi, b_hi], axis=0))
            + _dot(a_hi, b_lo))


def _silu(x):
    return x * jax.nn.sigmoid(x)


def _layer_norm(r, gain, bias):
    mu = jnp.mean(r, axis=-1, keepdims=True)
    d = r - mu
    var = jnp.mean(d * d, axis=-1, keepdims=True)
    return d * lax.rsqrt(var + LN_EPS) * gain + bias


def _rms_norm(x, gain):
    return x * lax.rsqrt(jnp.mean(x * x, axis=-1, keepdims=True) + RMS_EPS) * gain


def _proj_kernel(x_ref, *refs, out_scales):
    n = len(refs) // 2
    x = x_ref[...].astype(BF16)
    for w_ref, o_ref, scale in zip(refs[:n], refs[n:], out_scales):
        y = _dot(x, w_ref[...])
        o_ref[...] = (y if scale == 1.0 else y * scale).astype(o_ref.dtype)


def _proj(x, ws, out_dtypes, tm, out_scales=None):
    m, k = x.shape
    out_scales = tuple(out_scales) if out_scales is not None else (1.0,) * len(ws)
    in_specs = [pl.BlockSpec((tm, k), lambda i: (i, 0))]
    in_specs += [pl.BlockSpec(w.shape, lambda i: (0, 0)) for w in ws]
    out_specs = [pl.BlockSpec((tm, w.shape[1]), lambda i: (i, 0)) for w in ws]
    out_shape = [jax.ShapeDtypeStruct((m, w.shape[1]), dt) for w, dt in zip(ws, out_dtypes)]
    return pl.pallas_call(
        functools.partial(_proj_kernel, out_scales=out_scales), grid=(m // tm,), in_specs=in_specs, out_specs=out_specs, out_shape=out_shape,
        compiler_params=_cparams("parallel"), name="proj")(x, *ws)


def _out_ln_kernel(*refs, n_in, alpha):
    a_refs = refs[:n_in]
    w_refs = refs[n_in:2 * n_in]
    x_ref, g_ref, b_ref, o_ref = refs[2 * n_in:]
    y = None
    for a_ref, w_ref in zip(a_refs, w_refs):
        t = _dot(a_ref[...].astype(BF16), w_ref[...])
        y = t if y is None else y + t
    r = alpha * x_ref[...] + y
    o_ref[...] = _layer_norm(r, g_ref[...], b_ref[...])


def _out_ln(acts, ws, x, gain, bias, alpha, tm):
    m, d = x.shape
    n_in = len(acts)
    in_specs = [pl.BlockSpec((tm, a.shape[1]), lambda i: (i, 0)) for a in acts]
    in_specs += [pl.BlockSpec(w.shape, lambda i: (0, 0)) for w in ws]
    in_specs += [pl.BlockSpec((tm, d), lambda i: (i, 0)),
                 pl.BlockSpec((1, d), lambda i: (0, 0)), pl.BlockSpec((1, d), lambda i: (0, 0))]
    return pl.pallas_call(
        functools.partial(_out_ln_kernel, n_in=n_in, alpha=alpha),
        grid=(m // tm,), in_specs=in_specs, out_specs=pl.BlockSpec((tm, d), lambda i: (i, 0)),
        out_shape=jax.ShapeDtypeStruct((m, d), F32),
        compiler_params=_cparams("parallel"), name="out_ln")(*acts, *ws, x, gain.reshape(1, d), bias.reshape(1, d))


def _attn_kernel(*refs, tq, tkc, packed, has_bias):
    refs = list(refs)
    q_sc = refs.pop() if packed else None
    if has_bias:
        q_ref, k_ref, v_ref, cq_ref, ck_ref, o_ref, m_sc, acc_sc, cq_sc = refs
    else:
        q_ref, k_ref, v_ref, o_ref, m_sc, acc_sc = refs
    p_id = pl.program_id(1)
    i = pl.program_id(2)
    n_below = (i * tq) // tkc
    n_all = ((i + 1) * tq + tkc - 1) // tkc

    m_sc[...] = jnp.full(m_sc.shape, NEG_INF, F32)
    acc_sc[...] = jnp.zeros(acc_sc.shape, F32)
    lane_q = lax.broadcasted_iota(jnp.int32, (tq, LANES), 1)
    if has_bias:
        cq = cq_ref[...]
        for hh in range(2):
            col = jnp.sum(jnp.where(lane_q == 2 * p_id + hh, cq, 0.0), axis=1, keepdims=True)
            cq_sc[hh] = jnp.broadcast_to(col, (tq, LANES))
    lane_v = lax.broadcasted_iota(jnp.int32, (tkc, LANES), 1)
    if packed:
        q2 = q_ref[...]
        for hh in range(2):
            q_sc[hh] = jnp.where((lane_q >= 64 * hh) & (lane_q < 64 * (hh + 1)), q2, jnp.zeros_like(q2))
    n_blk = tkc // LANES

    def chunk(c, masked):
        k0 = pl.multiple_of(c * tkc, tkc)
        v = v_ref[pl.ds(k0, tkc), :]
        ones = jnp.ones_like(v)
        v_aug = (jnp.where(lane_v < 64, v, ones), jnp.where(lane_v < 64, ones, v))
        if masked:
            row = i * tq + lax.broadcasted_iota(jnp.int32, (tq, LANES), 0)
            col = k0 + lax.broadcasted_iota(jnp.int32, (tq, LANES), 1)

        def qk(hh):
            if packed:
                return _dot_nt(q_sc[hh], k_ref[pl.ds(k0, tkc), :])
            return _dot_nt(q_ref[:, hh * LANES:(hh + 1) * LANES], k_ref[pl.ds(k0, tkc), hh * LANES:(hh + 1) * LANES])

        logits = None if has_bias else [qk(0), qk(1)]
        for hh in range(2):
            s = qk(hh) if logits is None else logits[hh]
            if has_bias:
                ck_row = ck_ref[0, 0, hh:hh + 1, pl.ds(k0, tkc)]
            blocks = []
            for bi in range(n_blk):
                sb = s[:, bi * LANES:(bi + 1) * LANES]
                if has_bias:
                    sb = sb - ck_row[:, bi * LANES:(bi + 1) * LANES]
                if masked:
                    sb = jnp.where(col + bi * LANES <= row, sb, NEG_INF)
                blocks.append(sb)
            m_cur = functools.reduce(jnp.maximum, blocks)
            m_cur = jnp.broadcast_to(jnp.max(m_cur, axis=1, keepdims=True), (tq, LANES))
            if has_bias:
                m_cur = m_cur + cq_sc[hh]
            m_prev = m_sc[hh]
            m_new = jnp.maximum(m_prev, m_cur)
            alpha = jnp.exp2(m_prev - m_new)
            shift = m_new - cq_sc[hh] if has_bias else m_new
            if has_bias:
                p = jnp.concatenate([jnp.exp2((sb - shift).astype(BF16)) for sb in blocks], axis=1)
            else:
                p = jnp.concatenate([jnp.exp2(sb - shift).astype(BF16) for sb in blocks], axis=1)
            acc_sc[hh] = alpha * acc_sc[hh] + _dot(p, v_aug[hh])
            m_sc[hh] = m_new

    def below(c, carry):
        chunk(c, False)
        return carry

    def diagonal(c, carry):
        chunk(c, True)
        return carry

    lax.fori_loop(0, n_below, below, 0)
    lax.fori_loop(n_below, n_all, diagonal, 0)

    acc0 = acc_sc[0]
    acc1 = acc_sc[1]
    o0 = acc0 / pltpu.roll(acc0, 64, 1)
    o1 = acc1 / pltpu.roll(acc1, 64, 1)
    o_ref[...] = jnp.where(lane_q < 64, o0, o1).astype(o_ref.dtype)


def _attention(q_arr, q_off, k_arr, k_off, v_arr, v_off, n_heads, batch, seq, packed, out_dtype,
               cq=None, ck=None, tq=1024, tkc=512, flags=None):
    t = batch * seq
    hp = n_heads // 2
    qk_w = LANES if packed else 2 * LANES
    nq = seq // tq
    has_bias = cq is not None
    in_specs = [pl.BlockSpec((tq, qk_w), lambda b, p, i: (b * nq + i, q_off + p)),
                pl.BlockSpec((seq, qk_w), lambda b, p, i: (b, k_off + p)),
                pl.BlockSpec((seq, LANES), lambda b, p, i: (b, v_off + p))]
    args = [q_arr, k_arr, v_arr]
    scratch = [pltpu.VMEM((2, tq, LANES), F32), pltpu.VMEM((2, tq, LANES), F32)]
    if has_bias:
        in_specs += [pl.BlockSpec((tq, LANES), lambda b, p, i: (b * nq + i, 0)),
                     pl.BlockSpec((1, 1, 2, seq), lambda b, p, i: (b, p, 0, 0))]
        args += [cq, ck]
        scratch += [pltpu.VMEM((2, tq, LANES), F32)]
    if packed:
        scratch += [pltpu.VMEM((2, tq, LANES), q_arr.dtype)]
    return pl.pallas_call(
        functools.partial(_attn_kernel, tq=tq, tkc=tkc, packed=packed, has_bias=has_bias),
        grid=(batch, hp, nq), in_specs=in_specs,
        out_specs=pl.BlockSpec((tq, LANES), lambda b, p, i: (b * nq + i, p)),
        out_shape=jax.ShapeDtypeStruct((t, hp * LANES), out_dtype),
        scratch_shapes=scratch,
        compiler_params=_cparams("parallel", "parallel", "arbitrary", flags=flags), name="attention")(*args)


def _fox_gate_kernel(s_ref, fb_ref, cq_ref, ck_ref, carry, *, tm):
    @pl.when(pl.program_id(1) == 0)
    def _():
        carry[...] = jnp.zeros(carry.shape, F32)

    logf = jax.nn.log_sigmoid(s_ref[...] + fb_ref[...])
    row = lax.broadcasted_iota(jnp.int32, (tm, tm), 0)
    col = lax.broadcasted_iota(jnp.int32, (tm, tm), 1)
    tri = (col <= row).astype(F32)
    c = _dot(tri, logf, HIGHEST) + carry[0:1, :]
    carry[...] = jnp.broadcast_to(c[tm - 1:tm, :], carry.shape)
    c2 = c * LOG2E
    cq_ref[...] = c2
    ck_ref[0] = c2.T[0:8, :]


def _fox_gate(small, f_bias_row, batch, seq, tm=256):
    t = batch * seq
    n = seq // tm
    return pl.pallas_call(
        functools.partial(_fox_gate_kernel, tm=tm), grid=(batch, n),
        in_specs=[pl.BlockSpec((tm, LANES), lambda b, i: (b * n + i, 0)), pl.BlockSpec((1, LANES), lambda b, i: (0, 0))],
        out_specs=[pl.BlockSpec((tm, LANES), lambda b, i: (b * n + i, 0)), pl.BlockSpec((1, 8, tm), lambda b, i: (b, 0, i))],
        out_shape=[jax.ShapeDtypeStruct((t, LANES), F32), jax.ShapeDtypeStruct((batch, 8, seq), F32)],
        scratch_shapes=[pltpu.VMEM((8, LANES), F32)],
        compiler_params=_cparams("parallel", "arbitrary"), name="fox_gate")(small, f_bias_row)


CHUNKS_PER_STEP = 2


def _gdn_local_kernel(x_ref, halo_ref, s_ref, cw_ref, alog_ref, dtb_ref,
                      qe_ref, w_ref, u_ref, kt_ref, ol_ref, dl_ref,
                      y_sc, beta_sc, ld_sc, *, tm, tiles_per_batch):
    c_sz = GDN_CHUNK
    n_chunks = tm // c_sz
    first = (pl.program_id(0) % tiles_per_batch) == 0
    halo = jnp.where(first, 0.0, halo_ref[...])
    ext = jnp.concatenate([halo, x_ref[...]], axis=0)
    acc = None
    for jj in range(CONV_K):
        term = cw_ref[jj:jj + 1, :] * ext[8 - (CONV_K - 1) + jj: 8 - (CONV_K - 1) + jj + tm, :]
        acc = term if acc is None else acc + term
    y_sc[...] = _silu(acc)
    sm = s_ref[...]
    beta_sc[...] = jax.nn.sigmoid(sm)
    ld_sc[...] = -jnp.exp(alog_ref[...]) * jax.nn.softplus(sm + dtb_ref[...])

    r64 = lax.broadcasted_iota(jnp.int32, (c_sz, c_sz), 0)
    c64 = lax.broadcasted_iota(jnp.int32, (c_sz, c_sz), 1)
    tri64 = (c64 <= r64).astype(F32)
    lane64 = lax.broadcasted_iota(jnp.int32, (c_sz, LANES), 1) < 64
    rowp = lax.broadcasted_iota(jnp.int32, (LANES, LANES), 0)
    colp = lax.broadcasted_iota(jnp.int32, (LANES, LANES), 1)
    row_h1 = rowp >= 64
    same_head = row_h1 == (colp >= 64)
    ti = rowp % 64
    tj = colp % 64
    incl = same_head & (tj <= ti)
    strict = same_head & (tj < ti)
    eye = (rowp == colp).astype(F32)
    lane_h1_full = lax.broadcasted_iota(jnp.int32, (LANES, LANES), 1) >= 64
    scale = GDN_D ** -0.5

    def sel(a0, a1):
        return jnp.where(lane64, a0, a1)

    n_pairs = GDN_HEADS // 2

    def prepare(c, prep):
        r0 = pl.multiple_of(c * c_sz, c_sz)
        rows = pl.ds(r0, c_sz)
        bl = beta_sc[rows, :]
        g = _dot(tri64, ld_sc[rows, :], HIGHEST)
        g_t = jnp.concatenate([g, g], axis=0).T
        dl_row = []
        for p in range(n_pairs):
            h0, h1 = 2 * p, 2 * p + 1
            q2 = y_sc[rows, p * LANES:(p + 1) * LANES]
            k2 = y_sc[rows, GDN_WIDTH + p * LANES:GDN_WIDTH + (p + 1) * LANES]
            v2 = y_sc[rows, 2 * GDN_WIDTH + p * LANES:2 * GDN_WIDTH + (p + 1) * LANES]

            def l2n(x2):
                sq = x2 * x2
                s0 = jnp.sum(jnp.where(lane64, sq, 0.0), axis=1, keepdims=True)
                s1 = jnp.sum(jnp.where(lane64, 0.0, sq), axis=1, keepdims=True)
                return x2 * sel(lax.rsqrt(s0 + RMS_EPS), lax.rsqrt(s1 + RMS_EPS))

            kn2 = l2n(k2)
            qs2 = l2n(q2) * scale
            bcol2 = sel(bl[:, 8 + h0:9 + h0], bl[:, 8 + h1:9 + h1])
            g0c = g[:, 16 + h0:17 + h0]
            g1c = g[:, 16 + h1:17 + h1]
            gcol2 = sel(g0c, g1c)
            eg2 = jnp.exp(gcol2)
            kb2 = kn2 * bcol2
            vb2 = v2 * bcol2
            kbg2 = kb2 * eg2
            qdec2 = qs2 * eg2
            gl0 = g[c_sz - 1:c_sz, 16 + h0:17 + h0]
            gl1 = g[c_sz - 1:c_sz, 16 + h1:17 + h1]
            glast2 = sel(gl0, gl1)
            ktail2 = kn2 * jnp.exp(glast2 - gcol2)
            dl_row.append(jnp.exp(glast2[0:1, :]))

            gcol_p = jnp.concatenate([jnp.broadcast_to(g0c, (c_sz, LANES)), jnp.broadcast_to(g1c, (c_sz, LANES))], axis=0)
            grow_p = jnp.where(row_h1, g_t[16 + h1:17 + h1, :], g_t[16 + h0:17 + h0, :])
            gdiff = gcol_p - grow_p
            decay = jnp.where(incl, jnp.exp(jnp.where(incl, gdiff, 0.0)), 0.0)

            kk = jnp.concatenate([kn2, kn2], axis=0).astype(BF16)
            kbm = jnp.concatenate([jnp.where(lane64, kb2, 0.0), jnp.where(lane64, 0.0, kb2)], axis=0).astype(BF16)
            qm = jnp.concatenate([jnp.where(lane64, qs2, 0.0), jnp.where(lane64, 0.0, qs2)], axis=0).astype(BF16)
            kq = _dot_nt(jnp.concatenate([kbm, qm], axis=0), kk)
            lower = jnp.where(strict, kq[:LANES] * decay, 0.0)
            attn = jnp.where(incl, kq[LANES:] * decay, 0.0)
            rhs = jnp.concatenate([jnp.concatenate([vb2, vb2], axis=0), jnp.concatenate([kbg2, kbg2], axis=0)], axis=1)
            prep.append((qdec2, ktail2, attn.astype(BF16), rhs.astype(BF16), -lower, rows, p))
        dl_ref[pl.ds(c, 1), :] = jnp.concatenate(dl_row, axis=1)

    def group_body(j, carry):
        prep = []
        for cc in range(CHUNKS_PER_STEP):
            prepare(j * CHUNKS_PER_STEP + cc, prep)

        mpow = [pr[4] for pr in prep]
        tinv = [eye + m for m in mpow]
        mpow = [_dot3(m, m) for m in mpow]
        for level in range(4):
            both = [_dot3(m, jnp.concatenate([m, t], axis=1)) for m, t in zip(mpow, tinv)]
            mpow = [bo[:, :LANES] for bo in both]
            tinv = [t + bo[:, LANES:] for t, bo in zip(tinv, both)]
        tinv = [t + _dot3(m, t) for m, t in zip(mpow, tinv)]
        uws = [_dot(t.astype(BF16), pr[3]) for t, pr in zip(tinv, prep)]
        olqs = [_dot(pr[2], uw.astype(BF16)) for uw, pr in zip(uws, prep)]
        for pr, uw, olq in zip(prep, uws, olqs):
            qdec2, ktail2, rows, p = pr[0], pr[1], pr[5], pr[6]
            u2 = sel(uw[:c_sz, :LANES], uw[c_sz:, :LANES])
            w2 = sel(uw[:c_sz, LANES:], uw[c_sz:, LANES:])
            ol2 = sel(olq[:c_sz, :LANES], olq[c_sz:, :LANES])
            aw2 = sel(olq[:c_sz, LANES:], olq[c_sz:, LANES:])
            cols = slice(p * LANES, (p + 1) * LANES)
            qe_ref[rows, cols] = qdec2 - aw2
            w_ref[rows, cols] = w2
            u_ref[rows, cols] = u2
            kt_ref[rows, cols] = ktail2
            ol_ref[rows, cols] = ol2
        return carry

    lax.fori_loop(0, n_chunks // CHUNKS_PER_STEP, group_body, 0)


def _gdn_local(qkv_pre, small, conv_w, alog_row, dtb_row, batch, seq, tm=512):
    t = batch * seq
    w3 = 3 * GDN_WIDTH
    tiles_per_batch = seq // tm
    tok = lambda i: (i, 0)
    outs = [jax.ShapeDtypeStruct((t, GDN_WIDTH), F32)] * 5 + [jax.ShapeDtypeStruct((t // GDN_CHUNK, GDN_WIDTH), F32)]
    out_specs = [pl.BlockSpec((tm, GDN_WIDTH), tok)] * 5 + [pl.BlockSpec((tm // GDN_CHUNK, GDN_WIDTH), tok)]
    return pl.pallas_call(
        functools.partial(_gdn_local_kernel, tm=tm, tiles_per_batch=tiles_per_batch),
        grid=(t // tm,),
        in_specs=[pl.BlockSpec((tm, w3), tok),
                  pl.BlockSpec((8, w3), lambda i: (jnp.maximum(i * (tm // 8) - 1, 0), 0)),
                  pl.BlockSpec((tm, LANES), tok),
                  pl.BlockSpec((CONV_K, w3), lambda i: (0, 0)),
                  pl.BlockSpec((1, LANES), lambda i: (0, 0)), pl.BlockSpec((1, LANES), lambda i: (0, 0))],
        out_specs=out_specs, out_shape=outs,
        scratch_shapes=[pltpu.VMEM((tm, w3), F32), pltpu.VMEM((tm, LANES), F32), pltpu.VMEM((tm, LANES), F32)],
        compiler_params=_cparams("parallel"), name="gdn_local")(qkv_pre, qkv_pre, small, conv_w, alog_row, dtb_row)


def _gdn_scan_kernel(qe_ref, w_ref, u_ref, kt_ref, ol_ref, dl_ref, z_ref, nw_ref, o_ref, s_sc, o_sc, *, tm, batch):
    c_sz = GDN_CHUNK
    n_chunks = tm // c_sz
    gw = 4 * GDN_D

    @pl.when(pl.program_id(0) == 0)
    def _():
        s_sc[...] = jnp.zeros(s_sc.shape, F32)

    rg = lax.broadcasted_iota(jnp.int32, (gw, gw), 0) // GDN_D
    cg = lax.broadcasted_iota(jnp.int32, (gw, gw), 1) // GDN_D
    bd_mask = rg == cg

    def chunk_body(c, carry):
        r0 = pl.multiple_of(c * c_sz, c_sz)
        rows = pl.ds(r0, c_sz)
        for b in range(batch):
            dl = dl_ref[b, pl.ds(c, 1), :]
            for gi in range(GDN_HEADS // 4):
                cols = slice(gi * gw, (gi + 1) * gw)
                s = s_sc[b, gi]
                lhs = jnp.concatenate([qe_ref[b, rows, cols], w_ref[b, rows, cols]], axis=0).astype(BF16)
                x = _dot(lhs, s.astype(BF16))
                o_sc[b, rows, cols] = x[:c_sz] + ol_ref[b, rows, cols]
                v_new = u_ref[b, rows, cols] - x[c_sz:]
                upd = _dot_tn(kt_ref[b, rows, cols].astype(BF16), v_new.astype(BF16))
                s_sc[b, gi] = s * dl[:, cols] + jnp.where(bd_mask, upd, 0.0)
        return carry

    lax.fori_loop(0, n_chunks, chunk_body, 0)

    w5 = GDN_WIDTH
    rh = lax.broadcasted_iota(jnp.int32, (w5, w5), 0) // GDN_D
    ch = lax.broadcasted_iota(jnp.int32, (w5, w5), 1) // GDN_D
    ones_bd = (rh == ch).astype(BF16)
    for b in range(batch):
        o = o_sc[b]
        sq = o * o
        hi = sq.astype(BF16)
        mid = (sq - hi.astype(F32)).astype(BF16)
        lo = (sq - hi.astype(F32) - mid.astype(F32)).astype(BF16)
        ms = (_dot(hi, ones_bd) + _dot(mid, ones_bd) + _dot(lo, ones_bd)) * (1.0 / GDN_D)
        o_ref[b] = (o * lax.rsqrt(ms + RMS_EPS) * nw_ref[...] * _silu(z_ref[b])).astype(o_ref.dtype)


def _gdn_scan(qe, w, u, kt, ol, dl, z, nw_row, batch, seq, out_dtype, tm=512):
    r3 = lambda a: a.reshape(batch, seq, GDN_WIDTH)
    blk = pl.BlockSpec((batch, tm, GDN_WIDTH), lambda i: (0, i, 0))
    dl3 = dl.reshape(batch, seq // GDN_CHUNK, GDN_WIDTH)
    out = pl.pallas_call(
        functools.partial(_gdn_scan_kernel, tm=tm, batch=batch), grid=(seq // tm,),
        in_specs=[blk] * 5 + [pl.BlockSpec((batch, tm // GDN_CHUNK, GDN_WIDTH), lambda i: (0, i, 0)), blk,
                              pl.BlockSpec((1, GDN_WIDTH), lambda i: (0, 0))],
        out_specs=blk, out_shape=jax.ShapeDtypeStruct((batch, seq, GDN_WIDTH), out_dtype),
        scratch_shapes=[pltpu.VMEM((batch, GDN_HEADS // 4, 4 * GDN_D, 4 * GDN_D), F32),
                        pltpu.VMEM((batch, tm, GDN_WIDTH), F32)],
        compiler_params=_cparams("arbitrary"), name="gdn_scan")(r3(qe), r3(w), r3(u), r3(kt), r3(ol), dl3, r3(z), nw_row)
    return out.reshape(batch * seq, GDN_WIDTH)


def _rope_kernel(pos_ref, inv_ref, cos_ref, sin_ref):
    ang = pos_ref[...].astype(F32) * inv_ref[...]
    lane = lax.broadcasted_iota(jnp.int32, ang.shape, 1)
    cos_ref[...] = jnp.where(lane < MLA_NOPE, 1.0, jnp.where(lane < MLA_NOPE + MLA_ROPE, jnp.cos(ang), 0.0))
    sin_ref[...] = jnp.where(lane >= MLA_NOPE + MLA_ROPE, jnp.sin(ang), 0.0)


def _rope_tables(pos_col, inv_row, tm=512):
    t = pos_col.shape[0]
    blk = pl.BlockSpec((tm, LANES), lambda i: (i, 0))
    return pl.pallas_call(
        _rope_kernel, grid=(t // tm,),
        in_specs=[pl.BlockSpec((tm, 1), lambda i: (i, 0)), pl.BlockSpec((1, LANES), lambda i: (0, 0))],
        out_specs=[blk, blk], out_shape=[jax.ShapeDtypeStruct((t, LANES), F32)] * 2,
        compiler_params=_cparams("parallel"), name="rope_tables")(pos_col, inv_row)


def _rotary(blocks, cos, sin, n_heads):
    width = n_heads * LANES
    cos_t = jnp.concatenate([cos] * n_heads, axis=1) if n_heads > 1 else cos
    sin_t = jnp.concatenate([sin] * n_heads, axis=1) if n_heads > 1 else sin
    return blocks * cos_t + pltpu.roll(blocks * sin_t, width - MLA_ROPE, 1)


def _mla_q_kernel(cq_ref, g_ref, w_ref, cos_ref, sin_ref, o_ref):
    y = _rms_norm(cq_ref[...], g_ref[...])
    q = _dot(y.astype(BF16), w_ref[...])
    q = _rotary(q, cos_ref[...], sin_ref[...], MLA_HEADS)
    o_ref[...] = (q * ((MLA_NOPE + MLA_ROPE) ** -0.5 * LOG2E)).astype(o_ref.dtype)


def _mla_q(c_q, gain_row, w_q, cos, sin, tm=256):
    t = c_q.shape[0]
    width = MLA_HEADS * LANES
    tok = lambda i: (i, 0)
    fixed = lambda i: (0, 0)
    return pl.pallas_call(
        _mla_q_kernel, grid=(t // tm,),
        in_specs=[pl.BlockSpec((tm, MLA_Q_LORA), tok), pl.BlockSpec((1, MLA_Q_LORA), fixed),
                  pl.BlockSpec(w_q.shape, fixed), pl.BlockSpec((tm, LANES), tok), pl.BlockSpec((tm, LANES), tok)],
        out_specs=pl.BlockSpec((tm, width), tok), out_shape=jax.ShapeDtypeStruct((t, width), BF16),
        compiler_params=_cparams("parallel"), name="mla_q")(c_q, gain_row, w_q, cos, sin)


def _mla_kv_kernel(ckv_ref, g_ref, kr_ref, wk_ref, wv_ref, cos_ref, sin_ref, k_ref, v_ref):
    y = _rms_norm(ckv_ref[...], g_ref[...]).astype(BF16)
    kr = _rotary(kr_ref[...], cos_ref[...], sin_ref[...], 1)
    lane = lax.broadcasted_iota(jnp.int32, kr.shape, 1)
    kr = jnp.where((lane >= MLA_NOPE) & (lane < MLA_NOPE + MLA_ROPE), kr, 0.0)
    k = _dot(y, wk_ref[...]) + jnp.concatenate([kr] * MLA_HEADS, axis=1)
    k_ref[...] = k.astype(k_ref.dtype)
    v_ref[...] = _dot(y, wv_ref[...]).astype(v_ref.dtype)


def _mla_kv(c_kv, gain_row, kr_blk, w_k, w_v, cos, sin, tm=256):
    t = c_kv.shape[0]
    tok = lambda i: (i, 0)
    fixed = lambda i: (0, 0)
    return pl.pallas_call(
        _mla_kv_kernel, grid=(t // tm,),
        in_specs=[pl.BlockSpec((tm, MLA_KV_LORA), tok), pl.BlockSpec((1, MLA_KV_LORA), fixed),
                  pl.BlockSpec((tm, LANES), tok), pl.BlockSpec(w_k.shape, fixed), pl.BlockSpec(w_v.shape, fixed),
                  pl.BlockSpec((tm, LANES), tok), pl.BlockSpec((tm, LANES), tok)],
        out_specs=[pl.BlockSpec((tm, w_k.shape[1]), tok), pl.BlockSpec((tm, w_v.shape[1]), tok)],
        out_shape=[jax.ShapeDtypeStruct((t, w_k.shape[1]), BF16), jax.ShapeDtypeStruct((t, w_v.shape[1]), BF16)],
        compiler_params=_cparams("parallel"), name="mla_kv")(c_kv, gain_row, kr_blk, w_k, w_v, cos, sin)


def _router_kernel(x_ref, wt_ref, bias_ref, idx_ref, gate_ref, *, tm):
    w_hi, w_lo = _split_bf16(wt_ref[...])
    x_hi, x_lo = _split_bf16(x_ref[...])
    part = _dot_nt(jnp.concatenate([w_hi, w_lo], axis=0), x_hi)
    logits = part[:N_EXPERTS] + part[N_EXPERTS:] + _dot_nt(w_hi, x_lo)
    scores = jax.nn.sigmoid(logits)
    biased = scores + bias_ref[...]
    epg = EXPERTS_PER_GROUP
    iota = lax.broadcasted_iota(jnp.int32, (epg, tm), 0)
    best = None
    for gi in range(N_GROUPS):
        blk = biased[gi * epg:(gi + 1) * epg, :]
        raw = scores[gi * epg:(gi + 1) * epg, :]
        m1 = jnp.max(blk, axis=0, keepdims=True)
        i1 = jnp.min(jnp.where(blk == m1, iota, epg), axis=0, keepdims=True)
        blk2 = jnp.where(iota == i1, -jnp.inf, blk)
        m2 = jnp.max(blk2, axis=0, keepdims=True)
        i2 = jnp.min(jnp.where(blk2 == m2, iota, epg), axis=0, keepdims=True)
        s1 = jnp.sum(jnp.where(iota == i1, raw, 0.0), axis=0, keepdims=True)
        s2 = jnp.sum(jnp.where(iota == i2, raw, 0.0), axis=0, keepdims=True)
        cand = (m1 + m2, gi * epg + i1, gi * epg + i2, s1, s2)
        if best is None:
            best = cand
        else:
            better = cand[0] > best[0]
            best = tuple(jnp.where(better, c, b) for c, b in zip(cand, best))
    _, e1, e2, s1, s2 = best
    denom = s1 + s2
    idx_ref[...] = jnp.concatenate([e1, e2], axis=0)
    gate_ref[...] = jnp.concatenate([s1 / denom, s2 / denom], axis=0)


def _router(xt, router_wt, bias_col, tm=512):
    t, d = xt.shape
    return pl.pallas_call(
        functools.partial(_router_kernel, tm=tm), grid=(t // tm,),
        in_specs=[pl.BlockSpec((tm, d), lambda i: (i, 0)), pl.BlockSpec((N_EXPERTS, d), lambda i: (0, 0)),
                  pl.BlockSpec((N_EXPERTS, 1), lambda i: (0, 0))],
        out_specs=[pl.BlockSpec((TOP_K, tm), lambda i: (0, i)), pl.BlockSpec((TOP_K, tm), lambda i: (0, i))],
        out_shape=[jax.ShapeDtypeStruct((TOP_K, t), jnp.int32), jax.ShapeDtypeStruct((TOP_K, t), F32)],
        compiler_params=_cparams("parallel"), name="router")(xt, router_wt, bias_col)


def _moe_rank_kernel(idx_ref, start_ref, pos_ref, carry, *, tm):
    @pl.when(pl.program_id(0) == 0)
    def _():
        carry[...] = jnp.zeros(carry.shape, F32)

    idx = idx_ref[...]
    e_iota = lax.broadcasted_iota(jnp.int32, (N_EXPERTS, tm), 0)
    oh = [(e_iota == idx[kk:kk + 1, :]).astype(F32) for kk in range(TOP_K)]
    both = oh[0] + oh[1]
    r_io = lax.broadcasted_iota(jnp.int32, (tm, tm), 0)
    c_io = lax.broadcasted_iota(jnp.int32, (tm, tm), 1)
    before = (r_io < c_io).astype(BF16)
    base = _dot(both.astype(BF16), before) + carry[:, 0:1] + start_ref[...]
    pos = [jnp.sum(o * base, axis=0, keepdims=True) for o in oh]
    pos_ref[...] = jnp.concatenate(pos, axis=0).astype(jnp.int32)
    carry[...] = carry[...] + jnp.sum(both, axis=1, keepdims=True)


def _moe_rank(idx, start_col, tm=512):
    t = idx.shape[1]
    return pl.pallas_call(
        functools.partial(_moe_rank_kernel, tm=tm), grid=(t // tm,),
        in_specs=[pl.BlockSpec((TOP_K, tm), lambda i: (0, i)), pl.BlockSpec((N_EXPERTS, 1), lambda i: (0, 0))],
        out_specs=pl.BlockSpec((TOP_K, tm), lambda i: (0, i)),
        out_shape=jax.ShapeDtypeStruct((TOP_K, t), jnp.int32),
        scratch_shapes=[pltpu.VMEM((N_EXPERTS, LANES), F32)],
        compiler_params=_cparams("arbitrary"), name="moe_rank")(idx, start_col)


def _moe_dispatch_kernel(pos_ref, x_ref, zero_hbm, xs_hbm, stage, sem, *, tm, n_tok, n_steps):
    del zero_hbm
    i = pl.program_id(0)
    slot = i % 2
    base = i * tm

    def row_copy(sl, r, dst):
        return pltpu.make_async_copy(stage.at[sl, pl.ds(r, 1)], xs_hbm.at[pl.ds(dst, 1)], sem.at[sl])

    def drain(sl):
        for _ in range(TOP_K * tm):
            row_copy(sl, 0, 0).wait()

    @pl.when(i >= 2)
    def _():
        drain(slot)

    stage[slot] = x_ref[...]
    for kk in range(TOP_K):
        for r in range(tm):
            row_copy(slot, r, pos_ref[kk * n_tok + base + r]).start()

    @pl.when(i == n_steps - 1)
    def _():
        if n_steps > 1:
            drain(1 - slot)
        drain(slot)


def _moe_dispatch(xt, pos_flat, n_rows, tm=128):
    t, d = xt.shape
    grid_spec = pltpu.PrefetchScalarGridSpec(
        num_scalar_prefetch=1, grid=(t // tm,),
        in_specs=[pl.BlockSpec((tm, d), lambda i, pos: (i, 0)), pl.BlockSpec(memory_space=pl.ANY)],
        out_specs=pl.BlockSpec(memory_space=pl.ANY),
        scratch_shapes=[pltpu.VMEM((2, tm, d), xt.dtype), pltpu.SemaphoreType.DMA((2,))])
    return pl.pallas_call(
        functools.partial(_moe_dispatch_kernel, tm=tm, n_tok=t, n_steps=t // tm), grid_spec=grid_spec,
        out_shape=jax.ShapeDtypeStruct((n_rows, d), xt.dtype),
        input_output_aliases={2: 0},
        compiler_params=_cparams("arbitrary"), name="moe_dispatch")(pos_flat, xt, jnp.zeros((n_rows, d), xt.dtype))


def _moe_ffn_kernel(be_ref, nb_ref, r_ref, wg_ref, wu_ref, wd_ref, o_ref):
    del be_ref
    i = pl.program_id(0)

    @pl.when(i < nb_ref[0])
    def _():
        r = r_ref[...].astype(BF16)
        hidden = _silu(_dot(r, wg_ref[...])) * _dot(r, wu_ref[...])
        o_ref[...] = _dot(hidden.astype(BF16), wd_ref[...])

    @pl.when(i >= nb_ref[0])
    def _():
        o_ref[...] = jnp.zeros(o_ref.shape, o_ref.dtype)


def _moe_ffn(xs, block_expert, n_used, w_gate, w_up, w_down, tm):
    n_rows, d = xs.shape
    ff = w_gate.shape[2]
    grid_spec = pltpu.PrefetchScalarGridSpec(
        num_scalar_prefetch=2, grid=(n_rows // tm,),
        in_specs=[pl.BlockSpec((tm, d), lambda i, be, nb: (i, 0)),
                  pl.BlockSpec((None, d, ff), lambda i, be, nb: (be[i], 0, 0)),
                  pl.BlockSpec((None, d, ff), lambda i, be, nb: (be[i], 0, 0)),
                  pl.BlockSpec((None, ff, d), lambda i, be, nb: (be[i], 0, 0))],
        out_specs=pl.BlockSpec((tm, d), lambda i, be, nb: (i, 0)))
    return pl.pallas_call(
        _moe_ffn_kernel, grid_spec=grid_spec,
        out_shape=jax.ShapeDtypeStruct((n_rows, d), F32),
        compiler_params=_cparams("arbitrary"), name="moe_ffn")(block_expert, n_used, xs, w_gate, w_up, w_down)


def _combine_ln_kernel(pos_ref, rows_hbm, x_ref, gt_ref, g_ref, b_ref, o_ref, buf, sem, *, tm, n_tok, n_steps, alpha):
    i = pl.program_id(0)
    slot = i % 2

    def row_copy(src_row, sl, r):
        return pltpu.make_async_copy(rows_hbm.at[pl.ds(src_row, 1)], buf.at[sl, pl.ds(r, 1)], sem.at[sl])

    def start(step, sl):
        for kk in range(TOP_K):
            for r in range(tm):
                row_copy(pos_ref[kk * n_tok + step * tm + r], sl, kk * tm + r).start()

    @pl.when(i == 0)
    def _():
        start(0, 0)

    @pl.when(i + 1 < n_steps)
    def _():
        start(i + 1, 1 - slot)

    for r in range(TOP_K * tm):
        row_copy(0, slot, r).wait()
    gt = gt_ref[...]
    y = gt[:, 0:1] * buf[slot, 0:tm, :] + gt[:, 1:2] * buf[slot, tm:2 * tm, :]
    r = alpha * x_ref[...] + y
    o_ref[...] = _layer_norm(r, g_ref[...], b_ref[...])


def _combine_ln(out_rows, pos_flat, xt, gates_tk, gain, bias, alpha, tm=128):
    t, d = xt.shape
    n_steps = t // tm
    grid_spec = pltpu.PrefetchScalarGridSpec(
        num_scalar_prefetch=1, grid=(n_steps,),
        in_specs=[pl.BlockSpec(memory_space=pl.ANY),
                  pl.BlockSpec((tm, d), lambda i, pos: (i, 0)),
                  pl.BlockSpec((tm, TOP_K), lambda i, pos: (i, 0)),
                  pl.BlockSpec((1, d), lambda i, pos: (0, 0)), pl.BlockSpec((1, d), lambda i, pos: (0, 0))],
        out_specs=pl.BlockSpec((tm, d), lambda i, pos: (i, 0)),
        scratch_shapes=[pltpu.VMEM((2, TOP_K * tm, d), F32), pltpu.SemaphoreType.DMA((2,))])
    return pl.pallas_call(
        functools.partial(_combine_ln_kernel, tm=tm, n_tok=t, n_steps=n_steps, alpha=alpha), grid_spec=grid_spec,
        out_shape=jax.ShapeDtypeStruct((t, d), F32),
        compiler_params=_cparams("arbitrary"), name="combine_ln")(
            pos_flat, out_rows, xt, gates_tk, gain.reshape(1, d), bias.reshape(1, d))


MOE_TM = 256


def _moe_layer(xt, router_wt, bias_col, w_gate, w_up, w_down, gain, bias, alpha):
    t, d = xt.shape
    n_pairs = t * TOP_K
    n_blocks = n_pairs // MOE_TM + N_EXPERTS
    idx, gates = _router(xt, router_wt, bias_col)
    experts = jnp.arange(N_EXPERTS, dtype=jnp.int32)
    counts = jnp.sum((idx.reshape(n_pairs, 1) == experts[None, :]).astype(jnp.int32), axis=0)
    padded = (counts + MOE_TM - 1) // MOE_TM * MOE_TM
    padded_end = jnp.cumsum(padded)
    block_row0 = jnp.arange(n_blocks, dtype=jnp.int32) * MOE_TM
    block_expert = jnp.minimum(jnp.sum((padded_end[None, :] <= block_row0[:, None]).astype(jnp.int32), axis=1),
                               N_EXPERTS - 1).astype(jnp.int32)
    n_used = (padded_end[-1:] // MOE_TM).astype(jnp.int32)
    pos = _moe_rank(idx, (padded_end - padded).astype(F32).reshape(N_EXPERTS, 1))
    pos_flat = pos.reshape(n_pairs)
    xs = _moe_dispatch(xt, pos_flat, n_blocks * MOE_TM)
    out_rows = _moe_ffn(xs, block_expert, n_used, w_gate, w_up, w_down, MOE_TM)
    return _combine_ln(out_rows, pos_flat, xt, gates.T, gain, bias, alpha)


def _delta_fox_layer(xt, batch, seq, w_in, conv_w, a_log, dt_bias, norm_w, f_bias, w_out, gain, bias, alpha):
    o_qkv, o_z, o_beta, o_a, o_fox, o_f = 0, 1536, 2048, 2056, 2064, 3600
    zeros = lambda n: jnp.zeros((w_in.shape[0], n), w_in.dtype)
    w_small = jnp.concatenate([w_in[:, o_f:o_f + 8], w_in[:, o_beta:o_beta + 8], w_in[:, o_a:o_a + 8], zeros(LANES - 24)], axis=1)
    ws = [w_in[:, o_qkv:o_z], w_in[:, o_z:o_beta], w_small, w_in[:, o_fox:o_fox + FOX_WIDTH], w_in[:, o_fox + FOX_WIDTH:o_f]]
    qkv_pre, z, small, fox_q, fox_kv = _proj(
        xt, [w.astype(BF16) for w in ws], [F32, F32, F32, BF16, BF16], tm=256,
        out_scales=(1.0, 1.0, 1.0, FOX_D ** -0.5 * LOG2E, 1.0))

    def lane_row(v, off):
        return jnp.zeros((1, LANES), F32).at[0, off:off + v.shape[0]].set(v.astype(F32))

    qe, w, u, kt, ol, dl = _gdn_local(qkv_pre, small, conv_w.astype(F32), lane_row(a_log, 16), lane_row(dt_bias, 16), batch, seq)
    o_d = _gdn_scan(qe, w, u, kt, ol, dl, z, jnp.tile(norm_w.astype(F32), GDN_HEADS).reshape(1, GDN_WIDTH), batch, seq, BF16)
    cq, ck = _fox_gate(small, lane_row(f_bias, 0), batch, seq)
    ck = ck.reshape(batch, FOX_HEADS // 2, 2, seq)
    o_f = _attention(fox_q, 0, fox_kv, 0, fox_kv, FOX_WIDTH // LANES, FOX_HEADS, batch, seq, True, BF16, cq=cq, ck=ck)
    w_out = w_out.astype(BF16)
    return _out_ln([o_d, o_f], [w_out[:GDN_WIDTH], w_out[GDN_WIDTH:]], xt, gain, bias, alpha, tm=256)


def _mla_layer(xt, positions, batch, seq, w_in, q_norm, kv_norm, w_uq, w_ukv, w_out, gain, bias, alpha):
    t = xt.shape[0]
    half = MLA_ROPE // 2
    o_kv, o_kr = MLA_Q_LORA, MLA_Q_LORA + MLA_KV_LORA

    def rope_cols(w):
        return jnp.concatenate([w, -w[:, half:], w[:, :half]], axis=1)

    w_kr = jnp.concatenate([jnp.zeros((w_in.shape[0], MLA_NOPE), w_in.dtype), rope_cols(w_in[:, o_kr:])], axis=1)
    ws = [w_in[:, :o_kv], w_in[:, o_kv:o_kr], w_kr]
    c_q, c_kv, kr_blk = _proj(xt, [w.astype(BF16) for w in ws], [F32, F32, F32], tm=256)

    dq = MLA_NOPE + MLA_ROPE
    wq3 = w_uq.reshape(MLA_Q_LORA, MLA_HEADS, dq)
    wq_blk = jnp.concatenate([wq3, -wq3[:, :, MLA_NOPE + half:], wq3[:, :, MLA_NOPE:MLA_NOPE + half]], axis=2)
    wq_blk = wq_blk.reshape(MLA_Q_LORA, MLA_HEADS * LANES).astype(BF16)
    wkv3 = w_ukv.reshape(MLA_KV_LORA, MLA_HEADS, MLA_NOPE + MLA_V)
    wk_blk = jnp.concatenate([wkv3[:, :, :MLA_NOPE], jnp.zeros((MLA_KV_LORA, MLA_HEADS, LANES - MLA_NOPE), w_ukv.dtype)], axis=2)
    wk_blk = wk_blk.reshape(MLA_KV_LORA, MLA_HEADS * LANES).astype(BF16)
    wv_blk = wkv3[:, :, MLA_NOPE:].reshape(MLA_KV_LORA, MLA_HEADS * MLA_V).astype(BF16)

    inv_freq = ROPE_THETA ** (-jnp.arange(0, MLA_ROPE, 2, dtype=F32) / MLA_ROPE)
    inv_row = jnp.concatenate([jnp.zeros((MLA_NOPE,), F32)] + [inv_freq] * 4).reshape(1, LANES)
    cos, sin = _rope_tables(positions.reshape(t, 1).astype(jnp.int32), inv_row)

    q_blk = _mla_q(c_q, q_norm.astype(F32).reshape(1, -1), wq_blk, cos, sin)
    k_blk, v_blk = _mla_kv(c_kv, kv_norm.astype(F32).reshape(1, -1), kr_blk, wk_blk, wv_blk, cos, sin)
    o = _attention(q_blk, 0, k_blk, 0, v_blk, 0, MLA_HEADS, batch, seq, False, BF16)
    return _out_ln([o], [w_out.astype(BF16)], xt, gain, bias, alpha, tm=256)


def kernel(x, positions, ln_gain, ln_bias, router_w, router_bias, moe_w_gate, moe_w_up, moe_w_down, hy_w_in, hy_conv_w,
           gdn_a_log, gdn_dt_bias, gdn_norm_w, fox_f_bias, hy_w_out, mla_w_in, mla_q_norm, mla_kv_norm, mla_w_uq,
           mla_w_ukv, mla_w_out):
    batch, seq, d = x.shape
    depth = ln_gain.shape[0]
    alpha = (2.0 * depth) ** 0.25
    xt = x.reshape(batch * seq, d)
    router_wt = router_w.astype(F32).T
    bias_col = router_bias.astype(F32).reshape(N_EXPERTS, 1)
    for layer in range(depth):
        j = layer // 2
        if layer % 2 == 0:
            xt = _delta_fox_layer(xt, batch, seq, hy_w_in[j], hy_conv_w[j], gdn_a_log[j], gdn_dt_bias[j], gdn_norm_w[j],
                                  fox_f_bias[j], hy_w_out[j], ln_gain[layer, 0], ln_bias[layer, 0], alpha)
        else:
            xt = _mla_layer(xt, positions, batch, seq, mla_w_in[j], mla_q_norm[j], mla_kv_norm[j], mla_w_uq[j],
                            mla_w_ukv[j], mla_w_out[j], ln_gain[layer, 0], ln_bias[layer, 0], alpha)
        xt = _moe_layer(xt, router_wt, bias_col, moe_w_gate[layer].astype(BF16), moe_w_up[layer].astype(BF16),
                        moe_w_down[layer].astype(BF16), ln_gain[layer, 1], ln_bias[layer, 1], alpha)
    return xt.reshape(batch, seq, d)
```

```python
import functools
import math

import numpy as np
import jax
import jax.numpy as jnp
from jax import lax
from jax.experimental import pallas as pl
from jax.experimental.pallas import tpu as pltpu

F32 = jnp.float32
BF16 = jnp.bfloat16

D_MODEL = 1024
LN_EPS = 1e-5
RMS_EPS = 1e-6
NEG_INF = -1e30
LANES = 128

GDN_HEADS = 8
GDN_D = 64
GDN_CHUNK = 64
CONV_K = 4
GDN_WIDTH = GDN_HEADS * GDN_D
FOX_HEADS = 8
FOX_D = 64
FOX_WIDTH = FOX_HEADS * FOX_D
MLA_HEADS = 16
MLA_Q_LORA = 512
MLA_KV_LORA = 256
MLA_NOPE = 64
MLA_ROPE = 32
MLA_V = 64
ROPE_THETA = 10000.0
N_EXPERTS = 64
N_GROUPS = 8
EXPERTS_PER_GROUP = 8
TOP_K = 2
D_FF_EXPERT = 256

VMEM_LIMIT = 56 * 1024 * 1024
HIGHEST = lax.Precision.HIGHEST
LOG2E = 1.4426950408889634


def _cparams(*sem, flags=None):
    return pltpu.CompilerParams(dimension_semantics=sem, vmem_limit_bytes=VMEM_LIMIT, flags=flags)


def _dot(a, b, precision=None):
    return jnp.dot(a, b, preferred_element_type=F32, precision=precision)


def _dot_nt(a, b, precision=None):
    return lax.dot_general(a, b, (((1,), (1,)), ((), ())), preferred_element_type=F32, precision=precision)


def _dot_tn(a, b, precision=None):
    return lax.dot_general(a, b, (((0,), (0,)), ((), ())), preferred_element_type=F32, precision=precision)


def _split_bf16(x):
    hi = x.astype(BF16)
    return hi, (x - hi.astype(F32)).astype(BF16)


def _dot3(a, b):
    a_hi, a_lo = _split_bf16(a)
    b_hi, b_lo = _split_bf16(b)
    return (_dot(jnp.concatenate([a_hi, a_lo], axis=1), jnp.concatenate([b_hi, b_hi], axis=0))
            + _dot(a_hi, b_lo))


def _pack_halves(y):
    n = y.shape[1] // 2
    lo = lax.bitcast_convert_type(y[:, :n].astype(BF16).astype(F32), jnp.int32)
    hi = lax.bitcast_convert_type(y[:, n:].astype(BF16).astype(F32), jnp.int32)
    return hi | lax.shift_right_logical(lo, 16)


def _unpack_halves(packed):
    lo = lax.bitcast_convert_type(lax.shift_left(packed, 16), F32)
    hi = lax.bitcast_convert_type(packed & jnp.int32(-65536), F32)
    return jnp.concatenate([lo, hi], axis=1)


def _silu(x):
    return x * jax.nn.sigmoid(x)


def _layer_norm(r, gain, bias):
    mu = jnp.mean(r, axis=-1, keepdims=True)
    d = r - mu
    var = jnp.mean(d * d, axis=-1, keepdims=True)
    return d * lax.rsqrt(var + LN_EPS) * gain + bias


def _rms_norm(x, gain):
    return x * lax.rsqrt(jnp.mean(x * x, axis=-1, keepdims=True) + RMS_EPS) * gain


def _proj_kernel(x_ref, *refs, out_scales):
    n = len(refs) // 2
    x = x_ref[...].astype(BF16)
    for w_ref, o_ref, scale in zip(refs[:n], refs[n:], out_scales):
        y = _dot(x, w_ref[...])
        o_ref[...] = (y if scale == 1.0 else y * scale).astype(o_ref.dtype)


def _proj(x, ws, out_dtypes, tm, out_scales=None):
    m, k = x.shape
    out_scales = tuple(out_scales) if out_scales is not None else (1.0,) * len(ws)
    in_specs = [pl.BlockSpec((tm, k), lambda i: (i, 0))]
    in_specs += [pl.BlockSpec(w.shape, lambda i: (0, 0)) for w in ws]
    out_specs = [pl.BlockSpec((tm, w.shape[1]), lambda i: (i, 0)) for w in ws]
    out_shape = [jax.ShapeDtypeStruct((m, w.shape[1]), dt) for w, dt in zip(ws, out_dtypes)]
    return pl.pallas_call(
        functools.partial(_proj_kernel, out_scales=out_scales), grid=(m // tm,), in_specs=in_specs, out_specs=out_specs, out_shape=out_shape,
        compiler_params=_cparams("parallel"), name="proj")(x, *ws)


def _out_ln_kernel(*refs, n_in, alpha):
    a_refs = refs[:n_in]
    w_refs = refs[n_in:2 * n_in]
    x_ref, g_ref, b_ref, o_ref = refs[2 * n_in:]
    y = None
    for a_ref, w_ref in zip(a_refs, w_refs):
        t = _dot(a_ref[...].astype(BF16), w_ref[...])
        y = t if y is None else y + t
    r = alpha * x_ref[...] + y
    o_ref[...] = _layer_norm(r, g_ref[...], b_ref[...])


def _out_ln(acts, ws, x, gain, bias, alpha, tm):
    m, d = x.shape
    n_in = len(acts)
    in_specs = [pl.BlockSpec((tm, a.shape[1]), lambda i: (i, 0)) for a in acts]
    in_specs += [pl.BlockSpec(w.shape, lambda i: (0, 0)) for w in ws]
    in_specs += [pl.BlockSpec((tm, d), lambda i: (i, 0)),
                 pl.BlockSpec((1, d), lambda i: (0, 0)), pl.BlockSpec((1, d), lambda i: (0, 0))]
    return pl.pallas_call(
        functools.partial(_out_ln_kernel, n_in=n_in, alpha=alpha),
        grid=(m // tm,), in_specs=in_specs, out_specs=pl.BlockSpec((tm, d), lambda i: (i, 0)),
        out_shape=jax.ShapeDtypeStruct((m, d), F32),
        compiler_params=_cparams("parallel"), name="out_ln")(*acts, *ws, x, gain.reshape(1, d), bias.reshape(1, d))


def _attn_kernel(*refs, tq, tkc, packed, has_bias):
    refs = list(refs)
    q_sc = refs.pop() if packed else None
    if has_bias:
        q_ref, k_ref, v_ref, cq_ref, ck_ref, o_ref, m_sc, acc_sc, cq_sc = refs
    else:
        q_ref, k_ref, v_ref, o_ref, m_sc, acc_sc = refs
    p_id = pl.program_id(1)
    i = pl.program_id(2)
    assert tq == 2 * tkc
    n_below = 2 * i

    m_sc[...] = jnp.full(m_sc.shape, NEG_INF, F32)
    acc_sc[...] = jnp.zeros(acc_sc.shape, F32)
    lane_q = lax.broadcasted_iota(jnp.int32, (tq, LANES), 1)
    if has_bias:
        cq = cq_ref[...]
        for hh in range(2):
            col = jnp.sum(jnp.where(lane_q == 2 * p_id + hh, cq, 0.0), axis=1, keepdims=True)
            cq_sc[hh] = jnp.broadcast_to(col, (tq, LANES))
    lane_v = lax.broadcasted_iota(jnp.int32, (tkc, LANES), 1)
    if packed:
        q2 = q_ref[...]
        for hh in range(2):
            q_sc[hh] = jnp.where((lane_q >= 64 * hh) & (lane_q < 64 * (hh + 1)), q2, jnp.zeros_like(q2))
    n_blk = tkc // LANES

    def chunk(c, masked):
        k0 = pl.multiple_of(c * tkc, tkc)
        v = v_ref[pl.ds(k0, tkc), :]
        ones = jnp.ones_like(v)
        v_aug = (jnp.where(lane_v < 64, v, ones), jnp.where(lane_v < 64, ones, v))
        if masked:
            row = i * tq + lax.broadcasted_iota(jnp.int32, (tq, LANES), 0)
            col = k0 + lax.broadcasted_iota(jnp.int32, (tq, LANES), 1)

        def qk(hh):
            if packed:
                return _dot_nt(q_sc[hh], k_ref[pl.ds(k0, tkc), :])
            return _dot_nt(q_ref[:, hh * LANES:(hh + 1) * LANES], k_ref[pl.ds(k0, tkc), hh * LANES:(hh + 1) * LANES])

        logits = None if has_bias else [qk(0), qk(1)]
        for hh in range(2):
            s = qk(hh) if logits is None else logits[hh]
            if has_bias:
                ck_row = ck_ref[0, 0, hh:hh + 1, pl.ds(k0, tkc)]
            blocks = []
            for bi in range(n_blk):
                sb = s[:, bi * LANES:(bi + 1) * LANES]
                if has_bias:
                    sb = sb - ck_row[:, bi * LANES:(bi + 1) * LANES]
                if masked:
                    sb = jnp.where(col + bi * LANES <= row, sb, NEG_INF)
                blocks.append(sb)
            m_cur = functools.reduce(jnp.maximum, blocks)
            m_cur = jnp.broadcast_to(jnp.max(m_cur, axis=1, keepdims=True), (tq, LANES))
            if has_bias:
                m_cur = m_cur + cq_sc[hh]
            m_prev = m_sc[hh]
            m_new = jnp.maximum(m_prev, m_cur)
            alpha = jnp.exp2(m_prev - m_new)
            shift = m_new - cq_sc[hh] if has_bias else m_new
            if has_bias:
                p = jnp.concatenate([jnp.exp2((sb - shift).astype(BF16)) for sb in blocks], axis=1)
            else:
                p = jnp.concatenate([jnp.exp2(sb - shift).astype(BF16) for sb in blocks], axis=1)
            acc_sc[hh] = alpha * acc_sc[hh] + _dot(p, v_aug[hh])
            m_sc[hh] = m_new

    def below_pair(j, carry):
        chunk(2 * j, False)
        chunk(2 * j + 1, False)
        return carry

    lax.fori_loop(0, n_below // 2, below_pair, 0)
    chunk(n_below, True)
    chunk(n_below + 1, True)

    acc0 = acc_sc[0]
    acc1 = acc_sc[1]
    o0 = acc0 / pltpu.roll(acc0, 64, 1)
    o1 = acc1 / pltpu.roll(acc1, 64, 1)
    o_ref[...] = jnp.where(lane_q < 64, o0, o1).astype(o_ref.dtype)


def _attention(q_arr, q_off, k_arr, k_off, v_arr, v_off, n_heads, batch, seq, packed, out_dtype,
               cq=None, ck=None, tq=1024, tkc=512, flags=None):
    t = batch * seq
    hp = n_heads // 2
    qk_w = LANES if packed else 2 * LANES
    nq = seq // tq
    has_bias = cq is not None
    in_specs = [pl.BlockSpec((tq, qk_w), lambda b, p, i: (b * nq + i, q_off + p)),
                pl.BlockSpec((seq, qk_w), lambda b, p, i: (b, k_off + p)),
                pl.BlockSpec((seq, LANES), lambda b, p, i: (b, v_off + p))]
    args = [q_arr, k_arr, v_arr]
    scratch = [pltpu.VMEM((2, tq, LANES), F32), pltpu.VMEM((2, tq, LANES), F32)]
    if has_bias:
        in_specs += [pl.BlockSpec((tq, LANES), lambda b, p, i: (b * nq + i, 0)),
                     pl.BlockSpec((1, 1, 2, seq), lambda b, p, i: (b, p, 0, 0))]
        args += [cq, ck]
        scratch += [pltpu.VMEM((2, tq, LANES), F32)]
    if packed:
        scratch += [pltpu.VMEM((2, tq, LANES), q_arr.dtype)]
    return pl.pallas_call(
        functools.partial(_attn_kernel, tq=tq, tkc=tkc, packed=packed, has_bias=has_bias),
        grid=(batch, hp, nq), in_specs=in_specs,
        out_specs=pl.BlockSpec((tq, LANES), lambda b, p, i: (b * nq + i, p)),
        out_shape=jax.ShapeDtypeStruct((t, hp * LANES), out_dtype),
        scratch_shapes=scratch,
        compiler_params=_cparams("parallel", "parallel", "arbitrary", flags=flags), name="attention")(*args)


def _fox_gate_kernel(s_ref, fb_ref, cq_ref, ck_ref, carry, *, tm):
    @pl.when(pl.program_id(1) == 0)
    def _():
        carry[...] = jnp.zeros(carry.shape, F32)

    logf = jax.nn.log_sigmoid(s_ref[...] + fb_ref[...])
    row = lax.broadcasted_iota(jnp.int32, (tm, tm), 0)
    col = lax.broadcasted_iota(jnp.int32, (tm, tm), 1)
    tri = (col <= row).astype(F32)
    c = _dot(tri, logf, HIGHEST) + carry[0:1, :]
    carry[...] = jnp.broadcast_to(c[tm - 1:tm, :], carry.shape)
    c2 = c * LOG2E
    cq_ref[...] = c2
    ck_ref[0] = c2.T[0:8, :]


def _fox_gate(small, f_bias_row, batch, seq, tm=256):
    t = batch * seq
    n = seq // tm
    return pl.pallas_call(
        functools.partial(_fox_gate_kernel, tm=tm), grid=(batch, n),
        in_specs=[pl.BlockSpec((tm, LANES), lambda b, i: (b * n + i, 0)), pl.BlockSpec((1, LANES), lambda b, i: (0, 0))],
        out_specs=[pl.BlockSpec((tm, LANES), lambda b, i: (b * n + i, 0)), pl.BlockSpec((1, 8, tm), lambda b, i: (b, 0, i))],
        out_shape=[jax.ShapeDtypeStruct((t, LANES), F32), jax.ShapeDtypeStruct((batch, 8, seq), F32)],
        scratch_shapes=[pltpu.VMEM((8, LANES), F32)],
        compiler_params=_cparams("parallel", "arbitrary"), name="fox_gate")(small, f_bias_row)


CHUNKS_PER_STEP = 2


def _gdn_local_kernel(x_ref, halo_ref, s_ref, cw_ref, alog_ref, dtb_ref,
                      qe_ref, w_ref, u_ref, kt_ref, ol_ref, dl_ref,
                      y_sc, beta_sc, ld_sc, *, tm, tiles_per_batch):
    c_sz = GDN_CHUNK
    n_chunks = tm // c_sz
    first = (pl.program_id(0) % tiles_per_batch) == 0
    halo = jnp.where(first, 0.0, halo_ref[...])
    ext = jnp.concatenate([halo, x_ref[...]], axis=0)
    acc = None
    for jj in range(CONV_K):
        term = cw_ref[jj:jj + 1, :] * ext[8 - (CONV_K - 1) + jj: 8 - (CONV_K - 1) + jj + tm, :]
        acc = term if acc is None else acc + term
    y_sc[...] = _silu(acc)
    sm = s_ref[...]
    beta_sc[...] = jax.nn.sigmoid(sm)
    ld_sc[...] = -jnp.exp(alog_ref[...]) * jax.nn.softplus(sm + dtb_ref[...])

    r64 = lax.broadcasted_iota(jnp.int32, (c_sz, c_sz), 0)
    c64 = lax.broadcasted_iota(jnp.int32, (c_sz, c_sz), 1)
    tri64 = (c64 <= r64).astype(F32)
    lane64 = lax.broadcasted_iota(jnp.int32, (c_sz, LANES), 1) < 64
    rowp = lax.broadcasted_iota(jnp.int32, (LANES, LANES), 0)
    colp = lax.broadcasted_iota(jnp.int32, (LANES, LANES), 1)
    row_h1 = rowp >= 64
    same_head = row_h1 == (colp >= 64)
    ti = rowp % 64
    tj = colp % 64
    incl = same_head & (tj <= ti)
    strict = same_head & (tj < ti)
    eye = (rowp == colp).astype(F32)
    lane_h1_full = lax.broadcasted_iota(jnp.int32, (LANES, LANES), 1) >= 64
    scale = GDN_D ** -0.5

    def sel(a0, a1):
        return jnp.where(lane64, a0, a1)

    n_pairs = GDN_HEADS // 2

    def prepare(c, prep):
        r0 = pl.multiple_of(c * c_sz, c_sz)
        rows = pl.ds(r0, c_sz)
        bl = beta_sc[rows, :]
        g = _dot(tri64, ld_sc[rows, :], HIGHEST)
        g_t = jnp.concatenate([g, g], axis=0).T
        dl_row = []
        for p in range(n_pairs):
            h0, h1 = 2 * p, 2 * p + 1
            q2 = y_sc[rows, p * LANES:(p + 1) * LANES]
            k2 = y_sc[rows, GDN_WIDTH + p * LANES:GDN_WIDTH + (p + 1) * LANES]
            v2 = y_sc[rows, 2 * GDN_WIDTH + p * LANES:2 * GDN_WIDTH + (p + 1) * LANES]

            def l2n(x2):
                sq = x2 * x2
                s0 = jnp.sum(jnp.where(lane64, sq, 0.0), axis=1, keepdims=True)
                s1 = jnp.sum(jnp.where(lane64, 0.0, sq), axis=1, keepdims=True)
                return x2 * sel(lax.rsqrt(s0 + RMS_EPS), lax.rsqrt(s1 + RMS_EPS))

            kn2 = l2n(k2)
            qs2 = l2n(q2) * scale
            bcol2 = sel(bl[:, 8 + h0:9 + h0], bl[:, 8 + h1:9 + h1])
            g0c = g[:, 16 + h0:17 + h0]
            g1c = g[:, 16 + h1:17 + h1]
            gcol2 = sel(g0c, g1c)
            eg2 = jnp.exp(gcol2)
            kb2 = kn2 * bcol2
            vb2 = v2 * bcol2
            kbg2 = kb2 * eg2
            qdec2 = qs2 * eg2
            gl0 = g[c_sz - 1:c_sz, 16 + h0:17 + h0]
            gl1 = g[c_sz - 1:c_sz, 16 + h1:17 + h1]
            glast2 = sel(gl0, gl1)
            ktail2 = kn2 * jnp.exp(glast2 - gcol2)
            dl_row.append(jnp.exp(glast2[0:1, :]))

            gcol_p = jnp.concatenate([jnp.broadcast_to(g0c, (c_sz, LANES)), jnp.broadcast_to(g1c, (c_sz, LANES))], axis=0)
            grow_p = jnp.where(row_h1, g_t[16 + h1:17 + h1, :], g_t[16 + h0:17 + h0, :])
            gdiff = gcol_p - grow_p
            decay = jnp.where(incl, jnp.exp(jnp.where(incl, gdiff, 0.0)), 0.0)

            kk = jnp.concatenate([kn2, kn2], axis=0).astype(BF16)
            kbm = jnp.concatenate([jnp.where(lane64, kb2, 0.0), jnp.where(lane64, 0.0, kb2)], axis=0).astype(BF16)
            qm = jnp.concatenate([jnp.where(lane64, qs2, 0.0), jnp.where(lane64, 0.0, qs2)], axis=0).astype(BF16)
            kq = _dot_nt(jnp.concatenate([kbm, qm], axis=0), kk)
            lower = jnp.where(strict, kq[:LANES] * decay, 0.0)
            attn = jnp.where(incl, kq[LANES:] * decay, 0.0)
            rhs = jnp.concatenate([jnp.concatenate([vb2, vb2], axis=0), jnp.concatenate([kbg2, kbg2], axis=0)], axis=1)
            prep.append((qdec2, ktail2, attn.astype(BF16), rhs.astype(BF16), -lower, rows, p))
        dl_ref[pl.ds(c, 1), :] = jnp.concatenate(dl_row, axis=1)

    def group_body(j, carry):
        prep = []
        for cc in range(CHUNKS_PER_STEP):
            prepare(j * CHUNKS_PER_STEP + cc, prep)

        mpow = [pr[4] for pr in prep]
        tinv = [eye + m for m in mpow]
        mpow = [_dot3(m, m) for m in mpow]
        for level in range(4):
            both = [_dot3(m, jnp.concatenate([m, t], axis=1)) for m, t in zip(mpow, tinv)]
            mpow = [bo[:, :LANES] for bo in both]
            tinv = [t + bo[:, LANES:] for t, bo in zip(tinv, both)]
        tinv = [t + _dot3(m, t) for m, t in zip(mpow, tinv)]
        uws = [_dot(t.astype(BF16), pr[3]) for t, pr in zip(tinv, prep)]
        olqs = [_dot(pr[2], uw.astype(BF16)) for uw, pr in zip(uws, prep)]
        for pr, uw, olq in zip(prep, uws, olqs):
            qdec2, ktail2, rows, p = pr[0], pr[1], pr[5], pr[6]
            u2 = sel(uw[:c_sz, :LANES], uw[c_sz:, :LANES])
            w2 = sel(uw[:c_sz, LANES:], uw[c_sz:, LANES:])
            ol2 = sel(olq[:c_sz, :LANES], olq[c_sz:, :LANES])
            aw2 = sel(olq[:c_sz, LANES:], olq[c_sz:, LANES:])
            cols = slice(p * LANES, (p + 1) * LANES)
            qe_ref[rows, cols] = qdec2 - aw2
            w_ref[rows, cols] = w2
            u_ref[rows, cols] = u2
            kt_ref[rows, cols] = ktail2
            ol_ref[rows, cols] = ol2
        return carry

    lax.fori_loop(0, n_chunks // CHUNKS_PER_STEP, group_body, 0)


def _gdn_local(qkv_pre, small, conv_w, alog_row, dtb_row, batch, seq, tm=512):
    t = batch * seq
    w3 = 3 * GDN_WIDTH
    tiles_per_batch = seq // tm
    tok = lambda i: (i, 0)
    outs = [jax.ShapeDtypeStruct((t, GDN_WIDTH), F32)] * 5 + [jax.ShapeDtypeStruct((t // GDN_CHUNK, GDN_WIDTH), F32)]
    out_specs = [pl.BlockSpec((tm, GDN_WIDTH), tok)] * 5 + [pl.BlockSpec((tm // GDN_CHUNK, GDN_WIDTH), tok)]
    return pl.pallas_call(
        functools.partial(_gdn_local_kernel, tm=tm, tiles_per_batch=tiles_per_batch),
        grid=(t // tm,),
        in_specs=[pl.BlockSpec((tm, w3), tok),
                  pl.BlockSpec((8, w3), lambda i: (jnp.maximum(i * (tm // 8) - 1, 0), 0)),
                  pl.BlockSpec((tm, LANES), tok),
                  pl.BlockSpec((CONV_K, w3), lambda i: (0, 0)),
                  pl.BlockSpec((1, LANES), lambda i: (0, 0)), pl.BlockSpec((1, LANES), lambda i: (0, 0))],
        out_specs=out_specs, out_shape=outs,
        scratch_shapes=[pltpu.VMEM((tm, w3), F32), pltpu.VMEM((tm, LANES), F32), pltpu.VMEM((tm, LANES), F32)],
        compiler_params=_cparams("parallel"), name="gdn_local")(qkv_pre, qkv_pre, small, conv_w, alog_row, dtb_row)


def _gdn_scan_kernel(qe_ref, w_ref, u_ref, kt_ref, ol_ref, dl_ref, z_ref, nw_ref, o_ref, s_sc, o_sc, *, tm, batch):
    c_sz = GDN_CHUNK
    n_chunks = tm // c_sz
    gw = 4 * GDN_D

    @pl.when(pl.program_id(0) == 0)
    def _():
        s_sc[...] = jnp.zeros(s_sc.shape, F32)

    rg = lax.broadcasted_iota(jnp.int32, (gw, gw), 0) // GDN_D
    cg = lax.broadcasted_iota(jnp.int32, (gw, gw), 1) // GDN_D
    bd_mask = rg == cg

    def chunk_body(c, carry):
        r0 = pl.multiple_of(c * c_sz, c_sz)
        rows = pl.ds(r0, c_sz)
        for b in range(batch):
            dl = dl_ref[b, pl.ds(c, 1), :]
            for gi in range(GDN_HEADS // 4):
                cols = slice(gi * gw, (gi + 1) * gw)
                s = s_sc[b, gi]
                lhs = jnp.concatenate([qe_ref[b, rows, cols], w_ref[b, rows, cols]], axis=0).astype(BF16)
                x = _dot(lhs, s.astype(BF16))
                o_sc[b, rows, cols] = x[:c_sz] + ol_ref[b, rows, cols]
                v_new = u_ref[b, rows, cols] - x[c_sz:]
                upd = _dot_tn(kt_ref[b, rows, cols].astype(BF16), v_new.astype(BF16))
                s_sc[b, gi] = s * dl[:, cols] + jnp.where(bd_mask, upd, 0.0)
        return carry

    lax.fori_loop(0, n_chunks, chunk_body, 0)

    w5 = GDN_WIDTH
    rh = lax.broadcasted_iota(jnp.int32, (w5, w5), 0) // GDN_D
    ch = lax.broadcasted_iota(jnp.int32, (w5, w5), 1) // GDN_D
    ones_bd = (rh == ch).astype(BF16)
    for b in range(batch):
        o = o_sc[b]
        sq = o * o
        hi = sq.astype(BF16)
        mid = (sq - hi.astype(F32)).astype(BF16)
        lo = (sq - hi.astype(F32) - mid.astype(F32)).astype(BF16)
        ms = (_dot(hi, ones_bd) + _dot(mid, ones_bd) + _dot(lo, ones_bd)) * (1.0 / GDN_D)
        o_ref[b] = (o * lax.rsqrt(ms + RMS_EPS) * nw_ref[...] * _silu(z_ref[b])).astype(o_ref.dtype)


def _gdn_scan(qe, w, u, kt, ol, dl, z, nw_row, batch, seq, out_dtype, tm=512):
    r3 = lambda a: a.reshape(batch, seq, GDN_WIDTH)
    blk = pl.BlockSpec((batch, tm, GDN_WIDTH), lambda i: (0, i, 0))
    dl3 = dl.reshape(batch, seq // GDN_CHUNK, GDN_WIDTH)
    out = pl.pallas_call(
        functools.partial(_gdn_scan_kernel, tm=tm, batch=batch), grid=(seq // tm,),
        in_specs=[blk] * 5 + [pl.BlockSpec((batch, tm // GDN_CHUNK, GDN_WIDTH), lambda i: (0, i, 0)), blk,
                              pl.BlockSpec((1, GDN_WIDTH), lambda i: (0, 0))],
        out_specs=blk, out_shape=jax.ShapeDtypeStruct((batch, seq, GDN_WIDTH), out_dtype),
        scratch_shapes=[pltpu.VMEM((batch, GDN_HEADS // 4, 4 * GDN_D, 4 * GDN_D), F32),
                        pltpu.VMEM((batch, tm, GDN_WIDTH), F32)],
        compiler_params=_cparams("arbitrary"), name="gdn_scan")(r3(qe), r3(w), r3(u), r3(kt), r3(ol), dl3, r3(z), nw_row)
    return out.reshape(batch * seq, GDN_WIDTH)


def _rope_kernel(pos_ref, inv_ref, cos_ref, sin_ref):
    ang = pos_ref[...].astype(F32) * inv_ref[...]
    lane = lax.broadcasted_iota(jnp.int32, ang.shape, 1)
    cos_ref[...] = jnp.where(lane < MLA_NOPE, 1.0, jnp.where(lane < MLA_NOPE + MLA_ROPE, jnp.cos(ang), 0.0))
    sin_ref[...] = jnp.where(lane >= MLA_NOPE + MLA_ROPE, jnp.sin(ang), 0.0)


def _rope_tables(pos_col, inv_row, tm=512):
    t = pos_col.shape[0]
    blk = pl.BlockSpec((tm, LANES), lambda i: (i, 0))
    return pl.pallas_call(
        _rope_kernel, grid=(t // tm,),
        in_specs=[pl.BlockSpec((tm, 1), lambda i: (i, 0)), pl.BlockSpec((1, LANES), lambda i: (0, 0))],
        out_specs=[blk, blk], out_shape=[jax.ShapeDtypeStruct((t, LANES), F32)] * 2,
        compiler_params=_cparams("parallel"), name="rope_tables")(pos_col, inv_row)


def _rotary(blocks, cos, sin, n_heads):
    width = n_heads * LANES
    cos_t = jnp.concatenate([cos] * n_heads, axis=1) if n_heads > 1 else cos
    sin_t = jnp.concatenate([sin] * n_heads, axis=1) if n_heads > 1 else sin
    return blocks * cos_t + pltpu.roll(blocks * sin_t, width - MLA_ROPE, 1)


def _mla_q_kernel(cq_ref, g_ref, w_ref, cos_ref, sin_ref, o_ref):
    y = _rms_norm(cq_ref[...], g_ref[...])
    q = _dot(y.astype(BF16), w_ref[...])
    q = _rotary(q, cos_ref[...], sin_ref[...], MLA_HEADS)
    o_ref[...] = (q * ((MLA_NOPE + MLA_ROPE) ** -0.5 * LOG2E)).astype(o_ref.dtype)


def _mla_q(c_q, gain_row, w_q, cos, sin, tm=256):
    t = c_q.shape[0]
    width = MLA_HEADS * LANES
    tok = lambda i: (i, 0)
    fixed = lambda i: (0, 0)
    return pl.pallas_call(
        _mla_q_kernel, grid=(t // tm,),
        in_specs=[pl.BlockSpec((tm, MLA_Q_LORA), tok), pl.BlockSpec((1, MLA_Q_LORA), fixed),
                  pl.BlockSpec(w_q.shape, fixed), pl.BlockSpec((tm, LANES), tok), pl.BlockSpec((tm, LANES), tok)],
        out_specs=pl.BlockSpec((tm, width), tok), out_shape=jax.ShapeDtypeStruct((t, width), BF16),
        compiler_params=_cparams("parallel"), name="mla_q")(c_q, gain_row, w_q, cos, sin)


def _mla_kv_kernel(ckv_ref, g_ref, kr_ref, wk_ref, wv_ref, cos_ref, sin_ref, k_ref, v_ref):
    y = _rms_norm(ckv_ref[...], g_ref[...]).astype(BF16)
    kr = _rotary(kr_ref[...], cos_ref[...], sin_ref[...], 1)
    lane = lax.broadcasted_iota(jnp.int32, kr.shape, 1)
    kr = jnp.where((lane >= MLA_NOPE) & (lane < MLA_NOPE + MLA_ROPE), kr, 0.0)
    k = _dot(y, wk_ref[...]) + jnp.concatenate([kr] * MLA_HEADS, axis=1)
    k_ref[...] = k.astype(k_ref.dtype)
    v_ref[...] = _dot(y, wv_ref[...]).astype(v_ref.dtype)


def _mla_kv(c_kv, gain_row, kr_blk, w_k, w_v, cos, sin, tm=256):
    t = c_kv.shape[0]
    tok = lambda i: (i, 0)
    fixed = lambda i: (0, 0)
    return pl.pallas_call(
        _mla_kv_kernel, grid=(t // tm,),
        in_specs=[pl.BlockSpec((tm, MLA_KV_LORA), tok), pl.BlockSpec((1, MLA_KV_LORA), fixed),
                  pl.BlockSpec((tm, LANES), tok), pl.BlockSpec(w_k.shape, fixed), pl.BlockSpec(w_v.shape, fixed),
                  pl.BlockSpec((tm, LANES), tok), pl.BlockSpec((tm, LANES), tok)],
        out_specs=[pl.BlockSpec((tm, w_k.shape[1]), tok), pl.BlockSpec((tm, w_v.shape[1]), tok)],
        out_shape=[jax.ShapeDtypeStruct((t, w_k.shape[1]), BF16), jax.ShapeDtypeStruct((t, w_v.shape[1]), BF16)],
        compiler_params=_cparams("parallel"), name="mla_kv")(c_kv, gain_row, kr_blk, w_k, w_v, cos, sin)


def _router_kernel(x_ref, wt_ref, bias_ref, idx_ref, gate_ref, *, tm):
    w_hi, w_lo = _split_bf16(wt_ref[...])
    x_hi, x_lo = _split_bf16(x_ref[...])
    part = _dot_nt(jnp.concatenate([w_hi, w_lo], axis=0), x_hi)
    logits = part[:N_EXPERTS] + part[N_EXPERTS:] + _dot_nt(w_hi, x_lo)
    scores = jax.nn.sigmoid(logits)
    biased = scores + bias_ref[...]
    epg = EXPERTS_PER_GROUP
    iota = lax.broadcasted_iota(jnp.int32, (epg, tm), 0)
    best = None
    for gi in range(N_GROUPS):
        blk = biased[gi * epg:(gi + 1) * epg, :]
        raw = scores[gi * epg:(gi + 1) * epg, :]
        m1 = jnp.max(blk, axis=0, keepdims=True)
        i1 = jnp.min(jnp.where(blk == m1, iota, epg), axis=0, keepdims=True)
        blk2 = jnp.where(iota == i1, -jnp.inf, blk)
        m2 = jnp.max(blk2, axis=0, keepdims=True)
        i2 = jnp.min(jnp.where(blk2 == m2, iota, epg), axis=0, keepdims=True)
        s1 = jnp.sum(jnp.where(iota == i1, raw, 0.0), axis=0, keepdims=True)
        s2 = jnp.sum(jnp.where(iota == i2, raw, 0.0), axis=0, keepdims=True)
        cand = (m1 + m2, gi * epg + i1, gi * epg + i2, s1, s2)
        if best is None:
            best = cand
        else:
            better = cand[0] > best[0]
            best = tuple(jnp.where(better, c, b) for c, b in zip(cand, best))
    _, e1, e2, s1, s2 = best
    denom = s1 + s2
    idx_ref[...] = jnp.concatenate([e1, e2], axis=0)
    gate_ref[...] = jnp.concatenate([s1 / denom, s2 / denom], axis=0)


def _router(xt, router_wt, bias_col, tm=512):
    t, d = xt.shape
    return pl.pallas_call(
        functools.partial(_router_kernel, tm=tm), grid=(t // tm,),
        in_specs=[pl.BlockSpec((tm, d), lambda i: (i, 0)), pl.BlockSpec((N_EXPERTS, d), lambda i: (0, 0)),
                  pl.BlockSpec((N_EXPERTS, 1), lambda i: (0, 0))],
        out_specs=[pl.BlockSpec((TOP_K, tm), lambda i: (0, i)), pl.BlockSpec((TOP_K, tm), lambda i: (0, i))],
        out_shape=[jax.ShapeDtypeStruct((TOP_K, t), jnp.int32), jax.ShapeDtypeStruct((TOP_K, t), F32)],
        compiler_params=_cparams("parallel"), name="router")(xt, router_wt, bias_col)


def _moe_rank_kernel(idx_ref, start_ref, pos_ref, carry, *, tm):
    @pl.when(pl.program_id(0) == 0)
    def _():
        carry[...] = jnp.zeros(carry.shape, F32)

    idx = idx_ref[...]
    e_iota = lax.broadcasted_iota(jnp.int32, (N_EXPERTS, tm), 0)
    oh = [(e_iota == idx[kk:kk + 1, :]).astype(F32) for kk in range(TOP_K)]
    both = oh[0] + oh[1]
    r_io = lax.broadcasted_iota(jnp.int32, (tm, tm), 0)
    c_io = lax.broadcasted_iota(jnp.int32, (tm, tm), 1)
    before = (r_io < c_io).astype(BF16)
    base = _dot(both.astype(BF16), before) + carry[:, 0:1] + start_ref[...]
    pos = [jnp.sum(o * base, axis=0, keepdims=True) for o in oh]
    pos_ref[...] = jnp.concatenate(pos, axis=0).astype(jnp.int32)
    carry[...] = carry[...] + jnp.sum(both, axis=1, keepdims=True)


def _moe_rank(idx, start_col, tm=512):
    t = idx.shape[1]
    return pl.pallas_call(
        functools.partial(_moe_rank_kernel, tm=tm), grid=(t // tm,),
        in_specs=[pl.BlockSpec((TOP_K, tm), lambda i: (0, i)), pl.BlockSpec((N_EXPERTS, 1), lambda i: (0, 0))],
        out_specs=pl.BlockSpec((TOP_K, tm), lambda i: (0, i)),
        out_shape=jax.ShapeDtypeStruct((TOP_K, t), jnp.int32),
        scratch_shapes=[pltpu.VMEM((N_EXPERTS, LANES), F32)],
        compiler_params=_cparams("arbitrary"), name="moe_rank")(idx, start_col)


def _moe_dispatch_kernel(pos_ref, x_ref, zero_hbm, xs_hbm, stage, sem, *, tm, n_tok, n_steps):
    del zero_hbm
    i = pl.program_id(0)
    slot = i % 2
    base = i * tm

    def row_copy(sl, r, dst):
        return pltpu.make_async_copy(stage.at[sl, pl.ds(r, 1)], xs_hbm.at[pl.ds(dst, 1)], sem.at[sl])

    def drain(sl):
        for _ in range(TOP_K * tm):
            row_copy(sl, 0, 0).wait()

    @pl.when(i >= 2)
    def _():
        drain(slot)

    stage[slot] = _pack_halves(x_ref[...])
    for kk in range(TOP_K):
        for r in range(tm):
            row_copy(slot, r, pos_ref[kk * n_tok + base + r]).start()

    @pl.when(i == n_steps - 1)
    def _():
        if n_steps > 1:
            drain(1 - slot)
        drain(slot)


def _moe_dispatch(xt, pos_flat, n_rows, tm=128):
    t, d = xt.shape
    grid_spec = pltpu.PrefetchScalarGridSpec(
        num_scalar_prefetch=1, grid=(t // tm,),
        in_specs=[pl.BlockSpec((tm, d), lambda i, pos: (i, 0)), pl.BlockSpec(memory_space=pl.ANY)],
        out_specs=pl.BlockSpec(memory_space=pl.ANY),
        scratch_shapes=[pltpu.VMEM((2, tm, d // 2), jnp.int32), pltpu.SemaphoreType.DMA((2,))])
    return pl.pallas_call(
        functools.partial(_moe_dispatch_kernel, tm=tm, n_tok=t, n_steps=t // tm), grid_spec=grid_spec,
        out_shape=jax.ShapeDtypeStruct((n_rows, d // 2), jnp.int32),
        input_output_aliases={2: 0},
        compiler_params=_cparams("arbitrary"), name="moe_dispatch")(pos_flat, xt, jnp.zeros((n_rows, d // 2), jnp.int32))


def _moe_ffn_kernel(be_ref, nb_ref, r_ref, wg_ref, wu_ref, wd_ref, o_ref):
    del be_ref
    i = pl.program_id(0)

    @pl.when(i < nb_ref[0])
    def _():
        r = _unpack_halves(r_ref[...]).astype(BF16)
        hidden = _silu(_dot(r, wg_ref[...])) * _dot(r, wu_ref[...])
        o_ref[...] = _pack_halves(_dot(hidden.astype(BF16), wd_ref[...]))

    @pl.when(i >= nb_ref[0])
    def _():
        o_ref[...] = jnp.zeros(o_ref.shape, o_ref.dtype)


def _moe_ffn(xs, block_expert, n_used, w_gate, w_up, w_down, tm):
    n_rows, half = xs.shape
    d = 2 * half
    ff = w_gate.shape[2]
    grid_spec = pltpu.PrefetchScalarGridSpec(
        num_scalar_prefetch=2, grid=(n_rows // tm,),
        in_specs=[pl.BlockSpec((tm, half), lambda i, be, nb: (i, 0)),
                  pl.BlockSpec((None, d, ff), lambda i, be, nb: (be[i], 0, 0)),
                  pl.BlockSpec((None, d, ff), lambda i, be, nb: (be[i], 0, 0)),
                  pl.BlockSpec((None, ff, d), lambda i, be, nb: (be[i], 0, 0))],
        out_specs=pl.BlockSpec((tm, half), lambda i, be, nb: (i, 0)))
    return pl.pallas_call(
        _moe_ffn_kernel, grid_spec=grid_spec,
        out_shape=jax.ShapeDtypeStruct((n_rows, half), jnp.int32),
        compiler_params=_cparams("arbitrary"), name="moe_ffn")(block_expert, n_used, xs, w_gate, w_up, w_down)


def _combine_ln_kernel(pos_ref, rows_hbm, x_ref, gt_ref, g_ref, b_ref, o_ref, buf, sem, *, tm, n_tok, n_steps, alpha):
    i = pl.program_id(0)
    slot = i % 2

    def row_copy(src_row, sl, r):
        return pltpu.make_async_copy(rows_hbm.at[pl.ds(src_row, 1)], buf.at[sl, pl.ds(r, 1)], sem.at[sl])

    def start(step, sl):
        for kk in range(TOP_K):
            for r in range(tm):
                row_copy(pos_ref[kk * n_tok + step * tm + r], sl, kk * tm + r).start()

    @pl.when(i == 0)
    def _():
        start(0, 0)

    @pl.when(i + 1 < n_steps)
    def _():
        start(i + 1, 1 - slot)

    for r in range(TOP_K * tm):
        row_copy(0, slot, r).wait()
    gt = gt_ref[...]
    y = gt[:, 0:1] * _unpack_halves(buf[slot, 0:tm, :]) + gt[:, 1:2] * _unpack_halves(buf[slot, tm:2 * tm, :])
    r = alpha * x_ref[...] + y
    o_ref[...] = _layer_norm(r, g_ref[...], b_ref[...])


def _combine_ln(out_rows, pos_flat, xt, gates_tk, gain, bias, alpha, tm=128):
    t, d = xt.shape
    n_steps = t // tm
    grid_spec = pltpu.PrefetchScalarGridSpec(
        num_scalar_prefetch=1, grid=(n_steps,),
        in_specs=[pl.BlockSpec(memory_space=pl.ANY),
                  pl.BlockSpec((tm, d), lambda i, pos: (i, 0)),
                  pl.BlockSpec((tm, TOP_K), lambda i, pos: (i, 0)),
                  pl.BlockSpec((1, d), lambda i, pos: (0, 0)), pl.BlockSpec((1, d), lambda i, pos: (0, 0))],
        out_specs=pl.BlockSpec((tm, d), lambda i, pos: (i, 0)),
        scratch_shapes=[pltpu.VMEM((2, TOP_K * tm, d // 2), jnp.int32), pltpu.SemaphoreType.DMA((2,))])
    return pl.pallas_call(
        functools.partial(_combine_ln_kernel, tm=tm, n_tok=t, n_steps=n_steps, alpha=alpha), grid_spec=grid_spec,
        out_shape=jax.ShapeDtypeStruct((t, d), F32),
        compiler_params=_cparams("arbitrary"), name="combine_ln")(
            pos_flat, out_rows, xt, gates_tk, gain.reshape(1, d), bias.reshape(1, d))


MOE_TM = 256


def _moe_layer(xt, router_wt, bias_col, w_gate, w_up, w_down, gain, bias, alpha):
    t, d = xt.shape
    n_pairs = t * TOP_K
    n_blocks = n_pairs // MOE_TM + N_EXPERTS
    idx, gates = _router(xt, router_wt, bias_col)
    experts = jnp.arange(N_EXPERTS, dtype=jnp.int32)
    counts = jnp.sum((idx.reshape(n_pairs, 1) == experts[None, :]).astype(jnp.int32), axis=0)
    padded = (counts + MOE_TM - 1) // MOE_TM * MOE_TM
    padded_end = jnp.cumsum(padded)
    block_row0 = jnp.arange(n_blocks, dtype=jnp.int32) * MOE_TM
    block_expert = jnp.minimum(jnp.sum((padded_end[None, :] <= block_row0[:, None]).astype(jnp.int32), axis=1),
                               N_EXPERTS - 1).astype(jnp.int32)
    n_used = (padded_end[-1:] // MOE_TM).astype(jnp.int32)
    pos = _moe_rank(idx, (padded_end - padded).astype(F32).reshape(N_EXPERTS, 1))
    pos_flat = pos.reshape(n_pairs)
    xs = _moe_dispatch(xt, pos_flat, n_blocks * MOE_TM)
    out_rows = _moe_ffn(xs, block_expert, n_used, w_gate, w_up, w_down, MOE_TM)
    return _combine_ln(out_rows, pos_flat, xt, gates.T, gain, bias, alpha)


def _delta_fox_layer(xt, batch, seq, w_in, conv_w, a_log, dt_bias, norm_w, f_bias, w_out, gain, bias, alpha):
    o_qkv, o_z, o_beta, o_a, o_fox, o_f = 0, 1536, 2048, 2056, 2064, 3600
    zeros = lambda n: jnp.zeros((w_in.shape[0], n), w_in.dtype)
    w_small = jnp.concatenate([w_in[:, o_f:o_f + 8], w_in[:, o_beta:o_beta + 8], w_in[:, o_a:o_a + 8], zeros(LANES - 24)], axis=1)
    ws = [w_in[:, o_qkv:o_z], w_in[:, o_z:o_beta], w_small, w_in[:, o_fox:o_fox + FOX_WIDTH], w_in[:, o_fox + FOX_WIDTH:o_f]]
    qkv_pre, z, small, fox_q, fox_kv = _proj(
        xt, [w.astype(BF16) for w in ws], [F32, F32, F32, BF16, BF16], tm=256,
        out_scales=(1.0, 1.0, 1.0, FOX_D ** -0.5 * LOG2E, 1.0))

    def lane_row(v, off):
        return jnp.zeros((1, LANES), F32).at[0, off:off + v.shape[0]].set(v.astype(F32))

    qe, w, u, kt, ol, dl = _gdn_local(qkv_pre, small, conv_w.astype(F32), lane_row(a_log, 16), lane_row(dt_bias, 16), batch, seq)
    o_d = _gdn_scan(qe, w, u, kt, ol, dl, z, jnp.tile(norm_w.astype(F32), GDN_HEADS).reshape(1, GDN_WIDTH), batch, seq, BF16)
    cq, ck = _fox_gate(small, lane_row(f_bias, 0), batch, seq)
    ck = ck.reshape(batch, FOX_HEADS // 2, 2, seq)
    o_f = _attention(fox_q, 0, fox_kv, 0, fox_kv, FOX_WIDTH // LANES, FOX_HEADS, batch, seq, True, BF16, cq=cq, ck=ck)
    w_out = w_out.astype(BF16)
    return _out_ln([o_d, o_f], [w_out[:GDN_WIDTH], w_out[GDN_WIDTH:]], xt, gain, bias, alpha, tm=256)


def _mla_layer(xt, positions, batch, seq, w_in, q_norm, kv_norm, w_uq, w_ukv, w_out, gain, bias, alpha):
    t = xt.shape[0]
    half = MLA_ROPE // 2
    o_kv, o_kr = MLA_Q_LORA, MLA_Q_LORA + MLA_KV_LORA

    def rope_cols(w):
        return jnp.concatenate([w, -w[:, half:], w[:, :half]], axis=1)

    w_kr = jnp.concatenate([jnp.zeros((w_in.shape[0], MLA_NOPE), w_in.dtype), rope_cols(w_in[:, o_kr:])], axis=1)
    ws = [w_in[:, :o_kv], w_in[:, o_kv:o_kr], w_kr]
    c_q, c_kv, kr_blk = _proj(xt, [w.astype(BF16) for w in ws], [F32, F32, F32], tm=256)

    dq = MLA_NOPE + MLA_ROPE
    wq3 = w_uq.reshape(MLA_Q_LORA, MLA_HEADS, dq)
    wq_blk = jnp.concatenate([wq3, -wq3[:, :, MLA_NOPE + half:], wq3[:, :, MLA_NOPE:MLA_NOPE + half]], axis=2)
    wq_blk = wq_blk.reshape(MLA_Q_LORA, MLA_HEADS * LANES).astype(BF16)
    wkv3 = w_ukv.reshape(MLA_KV_LORA, MLA_HEADS, MLA_NOPE + MLA_V)
    wk_blk = jnp.concatenate([wkv3[:, :, :MLA_NOPE], jnp.zeros((MLA_KV_LORA, MLA_HEADS, LANES - MLA_NOPE), w_ukv.dtype)], axis=2)
    wk_blk = wk_blk.reshape(MLA_KV_LORA, MLA_HEADS * LANES).astype(BF16)
    wv_blk = wkv3[:, :, MLA_NOPE:].reshape(MLA_KV_LORA, MLA_HEADS * MLA_V).astype(BF16)

    inv_freq = ROPE_THETA ** (-jnp.arange(0, MLA_ROPE, 2, dtype=F32) / MLA_ROPE)
    inv_row = jnp.concatenate([jnp.zeros((MLA_NOPE,), F32)] + [inv_freq] * 4).reshape(1, LANES)
    cos, sin = _rope_tables(positions.reshape(t, 1).astype(jnp.int32), inv_row)

    q_blk = _mla_q(c_q, q_norm.astype(F32).reshape(1, -1), wq_blk, cos, sin)
    k_blk, v_blk = _mla_kv(c_kv, kv_norm.astype(F32).reshape(1, -1), kr_blk, wk_blk, wv_blk, cos, sin)
    o = _attention(q_blk, 0, k_blk, 0, v_blk, 0, MLA_HEADS, batch, seq, False, BF16)
    return _out_ln([o], [w_out.astype(BF16)], xt, gain, bias, alpha, tm=256)


def kernel(x, positions, ln_gain, ln_bias, router_w, router_bias, moe_w_gate, moe_w_up, moe_w_down, hy_w_in, hy_conv_w,
           gdn_a_log, gdn_dt_bias, gdn_norm_w, fox_f_bias, hy_w_out, mla_w_in, mla_q_norm, mla_kv_norm, mla_w_uq,
           mla_w_ukv, mla_w_out):
    batch, seq, d = x.shape
    depth = ln_gain.shape[0]
    alpha = (2.0 * depth) ** 0.25
    xt = x.reshape(batch * seq, d)
    router_wt = router_w.astype(F32).T
    bias_col = router_bias.astype(F32).reshape(N_EXPERTS, 1)
    for layer in range(depth):
        j = layer // 2
        if layer % 2 == 0:
            xt = _delta_fox_layer(xt, batch, seq, hy_w_in[j], hy_conv_w[j], gdn_a_log[j], gdn_dt_bias[j], gdn_norm_w[j],
                                  fox_f_bias[j], hy_w_out[j], ln_gain[layer, 0], ln_bias[layer, 0], alpha)
        else:
            xt = _mla_layer(xt, positions, batch, seq, mla_w_in[j], mla_q_norm[j], mla_kv_norm[j], mla_w_uq[j],
                            mla_w_ukv[j], mla_w_out[j], ln_gain[layer, 0], ln_bias[layer, 0], alpha)
        xt = _moe_layer(xt, router_wt, bias_col, moe_w_gate[layer].astype(BF16), moe_w_up[layer].astype(BF16),
                        moe_w_down[layer].astype(BF16), ln_gain[layer, 1], ln_bias[layer, 1], alpha)
    return xt.reshape(batch, seq, d)
```

```python
import functools
import math

import numpy as np
import jax
import jax.numpy as jnp
from jax import lax
from jax.experimental import pallas as pl
from jax.experimental.pallas import tpu as pltpu

F32 = jnp.float32
BF16 = jnp.bfloat16

D_MODEL = 1024
LN_EPS = 1e-5
RMS_EPS = 1e-6
NEG_INF = -1e30
LANES = 128

GDN_HEADS = 8
GDN_D = 64
GDN_CHUNK = 64
CONV_K = 4
GDN_WIDTH = GDN_HEADS * GDN_D
FOX_HEADS = 8
FOX_D = 64
FOX_WIDTH = FOX_HEADS * FOX_D
MLA_HEADS = 16
MLA_Q_LORA = 512
MLA_KV_LORA = 256
MLA_NOPE = 64
MLA_ROPE = 32
MLA_V = 64
ROPE_THETA = 10000.0
N_EXPERTS = 64
N_GROUPS = 8
EXPERTS_PER_GROUP = 8
TOP_K = 2
D_FF_EXPERT = 256

VMEM_LIMIT = 56 * 1024 * 1024
HIGHEST = lax.Precision.HIGHEST
LOG2E = 1.4426950408889634


def _cparams(*sem, flags=None):
    return pltpu.CompilerParams(dimension_semantics=sem, vmem_limit_bytes=VMEM_LIMIT, flags=flags)


def _dot(a, b, precision=None):
    return jnp.dot(a, b, preferred_element_type=F32, precision=precision)


def _dot_nt(a, b, precision=None):
    return lax.dot_general(a, b, (((1,), (1,)), ((), ())), preferred_element_type=F32, precision=precision)


def _dot_tn(a, b, precision=None):
    return lax.dot_general(a, b, (((0,), (0,)), ((), ())), preferred_element_type=F32, precision=precision)


def _split_bf16(x):
    hi = x.astype(BF16)
    return hi, (x - hi.astype(F32)).astype(BF16)


def _dot3(a, b):
    a_hi, a_lo = _split_bf16(a)
    b_hi, b_lo = _split_bf16(b)
    return (_dot(jnp.concatenate([a_hi, a_lo], axis=1), jnp.concatenate([b_hi, b_hi], axis=0))
            + _dot(a_hi, b_lo))


def _pack_halves(y):
    n = y.shape[1] // 2
    lo = lax.bitcast_convert_type(y[:, :n].astype(BF16).astype(F32), jnp.int32)
    hi = lax.bitcast_convert_type(y[:, n:].astype(BF16).astype(F32), jnp.int32)
    return hi | lax.shift_right_logical(lo, 16)


def _unpack_halves(packed):
    lo = lax.bitcast_convert_type(lax.shift_left(packed, 16), F32)
    hi = lax.bitcast_convert_type(packed & jnp.int32(-65536), F32)
    return jnp.concatenate([lo, hi], axis=1)


def _silu(x):
    return x * jax.nn.sigmoid(x)


def _layer_norm(r, gain, bias):
    mu = jnp.mean(r, axis=-1, keepdims=True)
    d = r - mu
    var = jnp.mean(d * d, axis=-1, keepdims=True)
    return d * lax.rsqrt(var + LN_EPS) * gain + bias


def _rms_norm(x, gain):
    return x * lax.rsqrt(jnp.mean(x * x, axis=-1, keepdims=True) + RMS_EPS) * gain


def _proj_kernel(x_ref, *refs, out_scales):
    n = len(refs) // 2
    x = x_ref[...].astype(BF16)
    for w_ref, o_ref, scale in zip(refs[:n], refs[n:], out_scales):
        y = _dot(x, w_ref[...])
        o_ref[...] = (y if scale == 1.0 else y * scale).astype(o_ref.dtype)


def _proj(x, ws, out_dtypes, tm, out_scales=None):
    m, k = x.shape
    out_scales = tuple(out_scales) if out_scales is not None else (1.0,) * len(ws)
    in_specs = [pl.BlockSpec((tm, k), lambda i: (i, 0))]
    in_specs += [pl.BlockSpec(w.shape, lambda i: (0, 0)) for w in ws]
    out_specs = [pl.BlockSpec((tm, w.shape[1]), lambda i: (i, 0)) for w in ws]
    out_shape = [jax.ShapeDtypeStruct((m, w.shape[1]), dt) for w, dt in zip(ws, out_dtypes)]
    return pl.pallas_call(
        functools.partial(_proj_kernel, out_scales=out_scales), grid=(m // tm,), in_specs=in_specs, out_specs=out_specs, out_shape=out_shape,
        compiler_params=_cparams("parallel"), name="proj")(x, *ws)


def _out_ln_kernel(*refs, n_in, alpha):
    a_refs = refs[:n_in]
    w_refs = refs[n_in:2 * n_in]
    x_ref, g_ref, b_ref, o_ref = refs[2 * n_in:]
    y = None
    for a_ref, w_ref in zip(a_refs, w_refs):
        t = _dot(a_ref[...].astype(BF16), w_ref[...])
        y = t if y is None else y + t
    r = alpha * x_ref[...] + y
    o_ref[...] = _layer_norm(r, g_ref[...], b_ref[...])


def _out_ln(acts, ws, x, gain, bias, alpha, tm):
    m, d = x.shape
    n_in = len(acts)
    in_specs = [pl.BlockSpec((tm, a.shape[1]), lambda i: (i, 0)) for a in acts]
    in_specs += [pl.BlockSpec(w.shape, lambda i: (0, 0)) for w in ws]
    in_specs += [pl.BlockSpec((tm, d), lambda i: (i, 0)),
                 pl.BlockSpec((1, d), lambda i: (0, 0)), pl.BlockSpec((1, d), lambda i: (0, 0))]
    return pl.pallas_call(
        functools.partial(_out_ln_kernel, n_in=n_in, alpha=alpha),
        grid=(m // tm,), in_specs=in_specs, out_specs=pl.BlockSpec((tm, d), lambda i: (i, 0)),
        out_shape=jax.ShapeDtypeStruct((m, d), F32),
        compiler_params=_cparams("parallel"), name="out_ln")(*acts, *ws, x, gain.reshape(1, d), bias.reshape(1, d))


def _attn_kernel(*refs, tq, tkc, packed, has_bias):
    refs = list(refs)
    q_sc = refs.pop() if packed else None
    if has_bias:
        q_ref, k_ref, v_ref, cq_ref, ck_ref, o_ref, m_sc, acc_sc, cq_sc = refs
    else:
        q_ref, k_ref, v_ref, o_ref, m_sc, acc_sc = refs
    p_id = pl.program_id(1)
    i = pl.program_id(2)
    assert tq == 2 * tkc
    n_below = 2 * i

    m_sc[...] = jnp.full(m_sc.shape, NEG_INF, F32)
    acc_sc[...] = jnp.zeros(acc_sc.shape, F32)
    lane_q = lax.broadcasted_iota(jnp.int32, (tq, LANES), 1)
    if has_bias:
        cq = cq_ref[...]
        for hh in range(2):
            col = jnp.sum(jnp.where(lane_q == 2 * p_id + hh, cq, 0.0), axis=1, keepdims=True)
            cq_sc[hh] = jnp.broadcast_to(col, (tq, LANES))
    lane_v = lax.broadcasted_iota(jnp.int32, (tkc, LANES), 1)
    if packed:
        q2 = q_ref[...]
        for hh in range(2):
            q_sc[hh] = jnp.where((lane_q >= 64 * hh) & (lane_q < 64 * (hh + 1)), q2, jnp.zeros_like(q2))
    n_blk = tkc // LANES

    def chunk(c, masked):
        k0 = pl.multiple_of(c * tkc, tkc)
        v = v_ref[pl.ds(k0, tkc), :]
        ones = jnp.ones_like(v)
        v_aug = (jnp.where(lane_v < 64, v, ones), jnp.where(lane_v < 64, ones, v))
        if masked:
            row = i * tq + lax.broadcasted_iota(jnp.int32, (tq, LANES), 0)
            col = k0 + lax.broadcasted_iota(jnp.int32, (tq, LANES), 1)

        def qk(hh):
            if packed:
                return _dot_nt(q_sc[hh], k_ref[pl.ds(k0, tkc), :])
            return _dot_nt(q_ref[:, hh * LANES:(hh + 1) * LANES], k_ref[pl.ds(k0, tkc), hh * LANES:(hh + 1) * LANES])

        logits = None if has_bias else [qk(0), qk(1)]
        for hh in range(2):
            s = qk(hh) if logits is None else logits[hh]
            if has_bias:
                ck_row = ck_ref[0, 0, hh:hh + 1, pl.ds(k0, tkc)]
            blocks = []
            for bi in range(n_blk):
                sb = s[:, bi * LANES:(bi + 1) * LANES]
                if has_bias:
                    sb = sb - ck_row[:, bi * LANES:(bi + 1) * LANES]
                if masked:
                    sb = jnp.where(col + bi * LANES <= row, sb, NEG_INF)
                blocks.append(sb)
            m_cur = functools.reduce(jnp.maximum, blocks)
            m_cur = jnp.broadcast_to(jnp.max(m_cur, axis=1, keepdims=True), (tq, LANES))
            if has_bias:
                m_cur = m_cur + cq_sc[hh]
            m_prev = m_sc[hh]
            m_new = jnp.maximum(m_prev, m_cur)
            alpha = jnp.exp2(m_prev - m_new)
            shift = m_new - cq_sc[hh] if has_bias else m_new
            if has_bias:
                p = jnp.concatenate([jnp.exp2((sb - shift).astype(BF16)) for sb in blocks], axis=1)
            else:
                p = jnp.concatenate([jnp.exp2(sb - shift).astype(BF16) for sb in blocks], axis=1)
            acc_sc[hh] = alpha * acc_sc[hh] + _dot(p, v_aug[hh])
            m_sc[hh] = m_new

    def below_quad(j, carry):
        for u in range(4):
            chunk(4 * j + u, False)
        return carry

    lax.fori_loop(0, n_below // 4, below_quad, 0)

    @pl.when(n_below % 4 == 2)
    def _():
        chunk(n_below - 2, False)
        chunk(n_below - 1, False)

    chunk(n_below, True)
    chunk(n_below + 1, True)

    acc0 = acc_sc[0]
    acc1 = acc_sc[1]
    o0 = acc0 / pltpu.roll(acc0, 64, 1)
    o1 = acc1 / pltpu.roll(acc1, 64, 1)
    o_ref[...] = jnp.where(lane_q < 64, o0, o1).astype(o_ref.dtype)


def _attention(q_arr, q_off, k_arr, k_off, v_arr, v_off, n_heads, batch, seq, packed, out_dtype,
               cq=None, ck=None, tq=1024, tkc=512, flags=None):
    t = batch * seq
    hp = n_heads // 2
    qk_w = LANES if packed else 2 * LANES
    nq = seq // tq
    has_bias = cq is not None
    in_specs = [pl.BlockSpec((tq, qk_w), lambda b, p, i: (b * nq + i, q_off + p)),
                pl.BlockSpec((seq, qk_w), lambda b, p, i: (b, k_off + p)),
                pl.BlockSpec((seq, LANES), lambda b, p, i: (b, v_off + p))]
    args = [q_arr, k_arr, v_arr]
    scratch = [pltpu.VMEM((2, tq, LANES), F32), pltpu.VMEM((2, tq, LANES), F32)]
    if has_bias:
        in_specs += [pl.BlockSpec((tq, LANES), lambda b, p, i: (b * nq + i, 0)),
                     pl.BlockSpec((1, 1, 2, seq), lambda b, p, i: (b, p, 0, 0))]
        args += [cq, ck]
        scratch += [pltpu.VMEM((2, tq, LANES), F32)]
    if packed:
        scratch += [pltpu.VMEM((2, tq, LANES), q_arr.dtype)]
    return pl.pallas_call(
        functools.partial(_attn_kernel, tq=tq, tkc=tkc, packed=packed, has_bias=has_bias),
        grid=(batch, hp, nq), in_specs=in_specs,
        out_specs=pl.BlockSpec((tq, LANES), lambda b, p, i: (b * nq + i, p)),
        out_shape=jax.ShapeDtypeStruct((t, hp * LANES), out_dtype),
        scratch_shapes=scratch,
        compiler_params=_cparams("parallel", "parallel", "arbitrary", flags=flags), name="attention")(*args)


def _fox_gate_kernel(s_ref, fb_ref, cq_ref, ck_ref, carry, *, tm):
    @pl.when(pl.program_id(1) == 0)
    def _():
        carry[...] = jnp.zeros(carry.shape, F32)

    logf = jax.nn.log_sigmoid(s_ref[...] + fb_ref[...])
    row = lax.broadcasted_iota(jnp.int32, (tm, tm), 0)
    col = lax.broadcasted_iota(jnp.int32, (tm, tm), 1)
    tri = (col <= row).astype(F32)
    c = _dot(tri, logf, HIGHEST) + carry[0:1, :]
    carry[...] = jnp.broadcast_to(c[tm - 1:tm, :], carry.shape)
    c2 = c * LOG2E
    cq_ref[...] = c2
    ck_ref[0] = c2.T[0:8, :]


def _fox_gate(small, f_bias_row, batch, seq, tm=256):
    t = batch * seq
    n = seq // tm
    return pl.pallas_call(
        functools.partial(_fox_gate_kernel, tm=tm), grid=(batch, n),
        in_specs=[pl.BlockSpec((tm, LANES), lambda b, i: (b * n + i, 0)), pl.BlockSpec((1, LANES), lambda b, i: (0, 0))],
        out_specs=[pl.BlockSpec((tm, LANES), lambda b, i: (b * n + i, 0)), pl.BlockSpec((1, 8, tm), lambda b, i: (b, 0, i))],
        out_shape=[jax.ShapeDtypeStruct((t, LANES), F32), jax.ShapeDtypeStruct((batch, 8, seq), F32)],
        scratch_shapes=[pltpu.VMEM((8, LANES), F32)],
        compiler_params=_cparams("parallel", "arbitrary"), name="fox_gate")(small, f_bias_row)


CHUNKS_PER_STEP = 2


def _gdn_local_kernel(x_ref, halo_ref, s_ref, cw_ref, alog_ref, dtb_ref,
                      qe_ref, w_ref, u_ref, kt_ref, ol_ref, dl_ref,
                      y_sc, beta_sc, ld_sc, *, tm, tiles_per_batch):
    c_sz = GDN_CHUNK
    n_chunks = tm // c_sz
    first = (pl.program_id(0) % tiles_per_batch) == 0
    halo = jnp.where(first, 0.0, halo_ref[...])
    ext = jnp.concatenate([halo, x_ref[...]], axis=0)
    acc = None
    for jj in range(CONV_K):
        term = cw_ref[jj:jj + 1, :] * ext[8 - (CONV_K - 1) + jj: 8 - (CONV_K - 1) + jj + tm, :]
        acc = term if acc is None else acc + term
    y_sc[...] = _silu(acc)
    sm = s_ref[...]
    beta_sc[...] = jax.nn.sigmoid(sm)
    ld_sc[...] = -jnp.exp(alog_ref[...]) * jax.nn.softplus(sm + dtb_ref[...])

    r64 = lax.broadcasted_iota(jnp.int32, (c_sz, c_sz), 0)
    c64 = lax.broadcasted_iota(jnp.int32, (c_sz, c_sz), 1)
    tri64 = (c64 <= r64).astype(F32)
    lane64 = lax.broadcasted_iota(jnp.int32, (c_sz, LANES), 1) < 64
    rowp = lax.broadcasted_iota(jnp.int32, (LANES, LANES), 0)
    colp = lax.broadcasted_iota(jnp.int32, (LANES, LANES), 1)
    row_h1 = rowp >= 64
    same_head = row_h1 == (colp >= 64)
    ti = rowp % 64
    tj = colp % 64
    incl = same_head & (tj <= ti)
    strict = same_head & (tj < ti)
    eye = (rowp == colp).astype(F32)
    lane_h1_full = lax.broadcasted_iota(jnp.int32, (LANES, LANES), 1) >= 64
    scale = GDN_D ** -0.5

    def sel(a0, a1):
        return jnp.where(lane64, a0, a1)

    n_pairs = GDN_HEADS // 2

    def prepare(c, prep):
        r0 = pl.multiple_of(c * c_sz, c_sz)
        rows = pl.ds(r0, c_sz)
        bl = beta_sc[rows, :]
        g = _dot(tri64, ld_sc[rows, :], HIGHEST)
        g_t = jnp.concatenate([g, g], axis=0).T
        dl_row = []
        for p in range(n_pairs):
            h0, h1 = 2 * p, 2 * p + 1
            q2 = y_sc[rows, p * LANES:(p + 1) * LANES]
            k2 = y_sc[rows, GDN_WIDTH + p * LANES:GDN_WIDTH + (p + 1) * LANES]
            v2 = y_sc[rows, 2 * GDN_WIDTH + p * LANES:2 * GDN_WIDTH + (p + 1) * LANES]

            def l2n(x2):
                sq = x2 * x2
                s0 = jnp.sum(jnp.where(lane64, sq, 0.0), axis=1, keepdims=True)
                s1 = jnp.sum(jnp.where(lane64, 0.0, sq), axis=1, keepdims=True)
                return x2 * sel(lax.rsqrt(s0 + RMS_EPS), lax.rsqrt(s1 + RMS_EPS))

            kn2 = l2n(k2)
            qs2 = l2n(q2) * scale
            bcol2 = sel(bl[:, 8 + h0:9 + h0], bl[:, 8 + h1:9 + h1])
            g0c = g[:, 16 + h0:17 + h0]
            g1c = g[:, 16 + h1:17 + h1]
            gcol2 = sel(g0c, g1c)
            eg2 = jnp.exp(gcol2)
            kb2 = kn2 * bcol2
            vb2 = v2 * bcol2
            kbg2 = kb2 * eg2
            qdec2 = qs2 * eg2
            gl0 = g[c_sz - 1:c_sz, 16 + h0:17 + h0]
            gl1 = g[c_sz - 1:c_sz, 16 + h1:17 + h1]
            glast2 = sel(gl0, gl1)
            ktail2 = kn2 * jnp.exp(glast2 - gcol2)
            dl_row.append(jnp.exp(glast2[0:1, :]))

            gcol_p = jnp.concatenate([jnp.broadcast_to(g0c, (c_sz, LANES)), jnp.broadcast_to(g1c, (c_sz, LANES))], axis=0)
            grow_p = jnp.where(row_h1, g_t[16 + h1:17 + h1, :], g_t[16 + h0:17 + h0, :])
            gdiff = gcol_p - grow_p
            decay = jnp.where(incl, jnp.exp(jnp.where(incl, gdiff, 0.0)), 0.0)

            kk = jnp.concatenate([kn2, kn2], axis=0).astype(BF16)
            kbm = jnp.concatenate([jnp.where(lane64, kb2, 0.0), jnp.where(lane64, 0.0, kb2)], axis=0).astype(BF16)
            qm = jnp.concatenate([jnp.where(lane64, qs2, 0.0), jnp.where(lane64, 0.0, qs2)], axis=0).astype(BF16)
            kq = _dot_nt(jnp.concatenate([kbm, qm], axis=0), kk)
            lower = jnp.where(strict, kq[:LANES] * decay, 0.0)
            attn = jnp.where(incl, kq[LANES:] * decay, 0.0)
            rhs = jnp.concatenate([jnp.concatenate([vb2, vb2], axis=0), jnp.concatenate([kbg2, kbg2], axis=0)], axis=1)
            prep.append((qdec2, ktail2, attn.astype(BF16), rhs.astype(BF16), -lower, rows, p))
        dl_ref[pl.ds(c, 1), :] = jnp.concatenate(dl_row, axis=1)

    def group_body(j, carry):
        prep = []
        for cc in range(CHUNKS_PER_STEP):
            prepare(j * CHUNKS_PER_STEP + cc, prep)

        mpow = [pr[4] for pr in prep]
        tinv = [eye + m for m in mpow]
        mpow = [_dot3(m, m) for m in mpow]
        for level in range(4):
            both = [_dot3(m, jnp.concatenate([m, t], axis=1)) for m, t in zip(mpow, tinv)]
            mpow = [bo[:, :LANES] for bo in both]
            tinv = [t + bo[:, LANES:] for t, bo in zip(tinv, both)]
        tinv = [t + _dot3(m, t) for m, t in zip(mpow, tinv)]
        uws = [_dot(t.astype(BF16), pr[3]) for t, pr in zip(tinv, prep)]
        olqs = [_dot(pr[2], uw.astype(BF16)) for uw, pr in zip(uws, prep)]
        for pr, uw, olq in zip(prep, uws, olqs):
            qdec2, ktail2, rows, p = pr[0], pr[1], pr[5], pr[6]
            u2 = sel(uw[:c_sz, :LANES], uw[c_sz:, :LANES])
            w2 = sel(uw[:c_sz, LANES:], uw[c_sz:, LANES:])
            ol2 = sel(olq[:c_sz, :LANES], olq[c_sz:, :LANES])
            aw2 = sel(olq[:c_sz, LANES:], olq[c_sz:, LANES:])
            cols = slice(p * LANES, (p + 1) * LANES)
            qe_ref[rows, cols] = qdec2 - aw2
            w_ref[rows, cols] = w2
            u_ref[rows, cols] = u2
            kt_ref[rows, cols] = ktail2
            ol_ref[rows, cols] = ol2
        return carry

    lax.fori_loop(0, n_chunks // CHUNKS_PER_STEP, group_body, 0)


def _gdn_local(qkv_pre, small, conv_w, alog_row, dtb_row, batch, seq, tm=512):
    t = batch * seq
    w3 = 3 * GDN_WIDTH
    tiles_per_batch = seq // tm
    tok = lambda i: (i, 0)
    outs = [jax.ShapeDtypeStruct((t, GDN_WIDTH), F32)] * 5 + [jax.ShapeDtypeStruct((t // GDN_CHUNK, GDN_WIDTH), F32)]
    out_specs = [pl.BlockSpec((tm, GDN_WIDTH), tok)] * 5 + [pl.BlockSpec((tm // GDN_CHUNK, GDN_WIDTH), tok)]
    return pl.pallas_call(
        functools.partial(_gdn_local_kernel, tm=tm, tiles_per_batch=tiles_per_batch),
        grid=(t // tm,),
        in_specs=[pl.BlockSpec((tm, w3), tok),
                  pl.BlockSpec((8, w3), lambda i: (jnp.maximum(i * (tm // 8) - 1, 0), 0)),
                  pl.BlockSpec((tm, LANES), tok),
                  pl.BlockSpec((CONV_K, w3), lambda i: (0, 0)),
                  pl.BlockSpec((1, LANES), lambda i: (0, 0)), pl.BlockSpec((1, LANES), lambda i: (0, 0))],
        out_specs=out_specs, out_shape=outs,
        scratch_shapes=[pltpu.VMEM((tm, w3), F32), pltpu.VMEM((tm, LANES), F32), pltpu.VMEM((tm, LANES), F32)],
        compiler_params=_cparams("parallel"), name="gdn_local")(qkv_pre, qkv_pre, small, conv_w, alog_row, dtb_row)


def _gdn_scan_kernel(qe_ref, w_ref, u_ref, kt_ref, ol_ref, dl_ref, z_ref, nw_ref, o_ref, s_sc, o_sc, *, tm, batch):
    c_sz = GDN_CHUNK
    n_chunks = tm // c_sz
    gw = 4 * GDN_D

    @pl.when(pl.program_id(0) == 0)
    def _():
        s_sc[...] = jnp.zeros(s_sc.shape, F32)

    rg = lax.broadcasted_iota(jnp.int32, (gw, gw), 0) // GDN_D
    cg = lax.broadcasted_iota(jnp.int32, (gw, gw), 1) // GDN_D
    bd_mask = rg == cg

    def chunk_body(c, carry):
        r0 = pl.multiple_of(c * c_sz, c_sz)
        rows = pl.ds(r0, c_sz)
        for b in range(batch):
            dl = dl_ref[b, pl.ds(c, 1), :]
            for gi in range(GDN_HEADS // 4):
                cols = slice(gi * gw, (gi + 1) * gw)
                s = s_sc[b, gi]
                lhs = jnp.concatenate([qe_ref[b, rows, cols], w_ref[b, rows, cols]], axis=0).astype(BF16)
                x = _dot(lhs, s.astype(BF16))
                o_sc[b, rows, cols] = x[:c_sz] + ol_ref[b, rows, cols]
                v_new = u_ref[b, rows, cols] - x[c_sz:]
                upd = _dot_tn(kt_ref[b, rows, cols].astype(BF16), v_new.astype(BF16))
                s_sc[b, gi] = s * dl[:, cols] + jnp.where(bd_mask, upd, 0.0)
        return carry

    lax.fori_loop(0, n_chunks, chunk_body, 0)

    w5 = GDN_WIDTH
    rh = lax.broadcasted_iota(jnp.int32, (w5, w5), 0) // GDN_D
    ch = lax.broadcasted_iota(jnp.int32, (w5, w5), 1) // GDN_D
    ones_bd = (rh == ch).astype(BF16)
    for b in range(batch):
        o = o_sc[b]
        sq = o * o
        hi = sq.astype(BF16)
        mid = (sq - hi.astype(F32)).astype(BF16)
        lo = (sq - hi.astype(F32) - mid.astype(F32)).astype(BF16)
        ms = (_dot(hi, ones_bd) + _dot(mid, ones_bd) + _dot(lo, ones_bd)) * (1.0 / GDN_D)
        o_ref[b] = (o * lax.rsqrt(ms + RMS_EPS) * nw_ref[...] * _silu(z_ref[b])).astype(o_ref.dtype)


def _gdn_scan(qe, w, u, kt, ol, dl, z, nw_row, batch, seq, out_dtype, tm=512):
    r3 = lambda a: a.reshape(batch, seq, GDN_WIDTH)
    blk = pl.BlockSpec((batch, tm, GDN_WIDTH), lambda i: (0, i, 0))
    dl3 = dl.reshape(batch, seq // GDN_CHUNK, GDN_WIDTH)
    out = pl.pallas_call(
        functools.partial(_gdn_scan_kernel, tm=tm, batch=batch), grid=(seq // tm,),
        in_specs=[blk] * 5 + [pl.BlockSpec((batch, tm // GDN_CHUNK, GDN_WIDTH), lambda i: (0, i, 0)), blk,
                              pl.BlockSpec((1, GDN_WIDTH), lambda i: (0, 0))],
        out_specs=blk, out_shape=jax.ShapeDtypeStruct((batch, seq, GDN_WIDTH), out_dtype),
        scratch_shapes=[pltpu.VMEM((batch, GDN_HEADS // 4, 4 * GDN_D, 4 * GDN_D), F32),
                        pltpu.VMEM((batch, tm, GDN_WIDTH), F32)],
        compiler_params=_cparams("arbitrary"), name="gdn_scan")(r3(qe), r3(w), r3(u), r3(kt), r3(ol), dl3, r3(z), nw_row)
    return out.reshape(batch * seq, GDN_WIDTH)


def _rope_kernel(pos_ref, inv_ref, cos_ref, sin_ref):
    ang = pos_ref[...].astype(F32) * inv_ref[...]
    lane = lax.broadcasted_iota(jnp.int32, ang.shape, 1)
    cos_ref[...] = jnp.where(lane < MLA_NOPE, 1.0, jnp.where(lane < MLA_NOPE + MLA_ROPE, jnp.cos(ang), 0.0))
    sin_ref[...] = jnp.where(lane >= MLA_NOPE + MLA_ROPE, jnp.sin(ang), 0.0)


def _rope_tables(pos_col, inv_row, tm=512):
    t = pos_col.shape[0]
    blk = pl.BlockSpec((tm, LANES), lambda i: (i, 0))
    return pl.pallas_call(
        _rope_kernel, grid=(t // tm,),
        in_specs=[pl.BlockSpec((tm, 1), lambda i: (i, 0)), pl.BlockSpec((1, LANES), lambda i: (0, 0))],
        out_specs=[blk, blk], out_shape=[jax.ShapeDtypeStruct((t, LANES), F32)] * 2,
        compiler_params=_cparams("parallel"), name="rope_tables")(pos_col, inv_row)


def _rotary(blocks, cos, sin, n_heads):
    width = n_heads * LANES
    cos_t = jnp.concatenate([cos] * n_heads, axis=1) if n_heads > 1 else cos
    sin_t = jnp.concatenate([sin] * n_heads, axis=1) if n_heads > 1 else sin
    return blocks * cos_t + pltpu.roll(blocks * sin_t, width - MLA_ROPE, 1)


def _mla_q_kernel(cq_ref, g_ref, w_ref, cos_ref, sin_ref, o_ref):
    y = _rms_norm(cq_ref[...], g_ref[...])
    q = _dot(y.astype(BF16), w_ref[...])
    q = _rotary(q, cos_ref[...], sin_ref[...], MLA_HEADS)
    o_ref[...] = (q * ((MLA_NOPE + MLA_ROPE) ** -0.5 * LOG2E)).astype(o_ref.dtype)


def _mla_q(c_q, gain_row, w_q, cos, sin, tm=256):
    t = c_q.shape[0]
    width = MLA_HEADS * LANES
    tok = lambda i: (i, 0)
    fixed = lambda i: (0, 0)
    return pl.pallas_call(
        _mla_q_kernel, grid=(t // tm,),
        in_specs=[pl.BlockSpec((tm, MLA_Q_LORA), tok), pl.BlockSpec((1, MLA_Q_LORA), fixed),
                  pl.BlockSpec(w_q.shape, fixed), pl.BlockSpec((tm, LANES), tok), pl.BlockSpec((tm, LANES), tok)],
        out_specs=pl.BlockSpec((tm, width), tok), out_shape=jax.ShapeDtypeStruct((t, width), BF16),
        compiler_params=_cparams("parallel"), name="mla_q")(c_q, gain_row, w_q, cos, sin)


def _mla_kv_kernel(ckv_ref, g_ref, kr_ref, wk_ref, wv_ref, cos_ref, sin_ref, k_ref, v_ref):
    y = _rms_norm(ckv_ref[...], g_ref[...]).astype(BF16)
    kr = _rotary(kr_ref[...], cos_ref[...], sin_ref[...], 1)
    lane = lax.broadcasted_iota(jnp.int32, kr.shape, 1)
    kr = jnp.where((lane >= MLA_NOPE) & (lane < MLA_NOPE + MLA_ROPE), kr, 0.0)
    k = _dot(y, wk_ref[...]) + jnp.concatenate([kr] * MLA_HEADS, axis=1)
    k_ref[...] = k.astype(k_ref.dtype)
    v_ref[...] = _dot(y, wv_ref[...]).astype(v_ref.dtype)


def _mla_kv(c_kv, gain_row, kr_blk, w_k, w_v, cos, sin, tm=256):
    t = c_kv.shape[0]
    tok = lambda i: (i, 0)
    fixed = lambda i: (0, 0)
    return pl.pallas_call(
        _mla_kv_kernel, grid=(t // tm,),
        in_specs=[pl.BlockSpec((tm, MLA_KV_LORA), tok), pl.BlockSpec((1, MLA_KV_LORA), fixed),
                  pl.BlockSpec((tm, LANES), tok), pl.BlockSpec(w_k.shape, fixed), pl.BlockSpec(w_v.shape, fixed),
                  pl.BlockSpec((tm, LANES), tok), pl.BlockSpec((tm, LANES), tok)],
        out_specs=[pl.BlockSpec((tm, w_k.shape[1]), tok), pl.BlockSpec((tm, w_v.shape[1]), tok)],
        out_shape=[jax.ShapeDtypeStruct((t, w_k.shape[1]), BF16), jax.ShapeDtypeStruct((t, w_v.shape[1]), BF16)],
        compiler_params=_cparams("parallel"), name="mla_kv")(c_kv, gain_row, kr_blk, w_k, w_v, cos, sin)


def _router_kernel(x_ref, wt_ref, bias_ref, idx_ref, gate_ref, *, tm):
    w_hi, w_lo = _split_bf16(wt_ref[...])
    x_hi, x_lo = _split_bf16(x_ref[...])
    part = _dot_nt(jnp.concatenate([w_hi, w_lo], axis=0), x_hi)
    logits = part[:N_EXPERTS] + part[N_EXPERTS:] + _dot_nt(w_hi, x_lo)
    scores = jax.nn.sigmoid(logits)
    biased = scores + bias_ref[...]
    epg = EXPERTS_PER_GROUP
    iota = lax.broadcasted_iota(jnp.int32, (epg, tm), 0)
    best = None
    for gi in range(N_GROUPS):
        blk = biased[gi * epg:(gi + 1) * epg, :]
        raw = scores[gi * epg:(gi + 1) * epg, :]
        m1 = jnp.max(blk, axis=0, keepdims=True)
        i1 = jnp.min(jnp.where(blk == m1, iota, epg), axis=0, keepdims=True)
        blk2 = jnp.where(iota == i1, -jnp.inf, blk)
        m2 = jnp.max(blk2, axis=0, keepdims=True)
        i2 = jnp.min(jnp.where(blk2 == m2, iota, epg), axis=0, keepdims=True)
        s1 = jnp.sum(jnp.where(iota == i1, raw, 0.0), axis=0, keepdims=True)
        s2 = jnp.sum(jnp.where(iota == i2, raw, 0.0), axis=0, keepdims=True)
        cand = (m1 + m2, gi * epg + i1, gi * epg + i2, s1, s2)
        if best is None:
            best = cand
        else:
            better = cand[0] > best[0]
            best = tuple(jnp.where(better, c, b) for c, b in zip(cand, best))
    _, e1, e2, s1, s2 = best
    denom = s1 + s2
    idx_ref[...] = jnp.concatenate([e1, e2], axis=0)
    gate_ref[...] = jnp.concatenate([s1 / denom, s2 / denom], axis=0)


def _router(xt, router_wt, bias_col, tm=512):
    t, d = xt.shape
    return pl.pallas_call(
        functools.partial(_router_kernel, tm=tm), grid=(t // tm,),
        in_specs=[pl.BlockSpec((tm, d), lambda i: (i, 0)), pl.BlockSpec((N_EXPERTS, d), lambda i: (0, 0)),
                  pl.BlockSpec((N_EXPERTS, 1), lambda i: (0, 0))],
        out_specs=[pl.BlockSpec((TOP_K, tm), lambda i: (0, i)), pl.BlockSpec((TOP_K, tm), lambda i: (0, i))],
        out_shape=[jax.ShapeDtypeStruct((TOP_K, t), jnp.int32), jax.ShapeDtypeStruct((TOP_K, t), F32)],
        compiler_params=_cparams("parallel"), name="router")(xt, router_wt, bias_col)


def _moe_rank_kernel(idx_ref, start_ref, pos_ref, carry, *, tm):
    @pl.when(pl.program_id(0) == 0)
    def _():
        carry[...] = jnp.zeros(carry.shape, F32)

    idx = idx_ref[...]
    e_iota = lax.broadcasted_iota(jnp.int32, (N_EXPERTS, tm), 0)
    oh = [(e_iota == idx[kk:kk + 1, :]).astype(F32) for kk in range(TOP_K)]
    both = oh[0] + oh[1]
    r_io = lax.broadcasted_iota(jnp.int32, (tm, tm), 0)
    c_io = lax.broadcasted_iota(jnp.int32, (tm, tm), 1)
    before = (r_io < c_io).astype(BF16)
    base = _dot(both.astype(BF16), before) + carry[:, 0:1] + start_ref[...]
    pos = [jnp.sum(o * base, axis=0, keepdims=True) for o in oh]
    pos_ref[...] = jnp.concatenate(pos, axis=0).astype(jnp.int32)
    carry[...] = carry[...] + jnp.sum(both, axis=1, keepdims=True)


def _moe_rank(idx, start_col, tm=512):
    t = idx.shape[1]
    return pl.pallas_call(
        functools.partial(_moe_rank_kernel, tm=tm), grid=(t // tm,),
        in_specs=[pl.BlockSpec((TOP_K, tm), lambda i: (0, i)), pl.BlockSpec((N_EXPERTS, 1), lambda i: (0, 0))],
        out_specs=pl.BlockSpec((TOP_K, tm), lambda i: (0, i)),
        out_shape=jax.ShapeDtypeStruct((TOP_K, t), jnp.int32),
        scratch_shapes=[pltpu.VMEM((N_EXPERTS, LANES), F32)],
        compiler_params=_cparams("arbitrary"), name="moe_rank")(idx, start_col)


def _moe_dispatch_kernel(pos_ref, x_ref, zero_hbm, xs_hbm, stage, sem, *, tm, n_tok, n_steps):
    del zero_hbm
    i = pl.program_id(0)
    slot = i % 2
    base = i * tm

    def row_copy(sl, r, dst):
        return pltpu.make_async_copy(stage.at[sl, pl.ds(r, 1)], xs_hbm.at[pl.ds(dst, 1)], sem.at[sl])

    def drain(sl):
        for _ in range(TOP_K * tm):
            row_copy(sl, 0, 0).wait()

    @pl.when(i >= 2)
    def _():
        drain(slot)

    stage[slot] = _pack_halves(x_ref[...])
    for kk in range(TOP_K):
        for r in range(tm):
            row_copy(slot, r, pos_ref[kk * n_tok + base + r]).start(priority=r % 2)

    @pl.when(i == n_steps - 1)
    def _():
        if n_steps > 1:
            drain(1 - slot)
        drain(slot)


def _moe_dispatch(xt, pos_flat, n_rows, tm=128):
    t, d = xt.shape
    grid_spec = pltpu.PrefetchScalarGridSpec(
        num_scalar_prefetch=1, grid=(t // tm,),
        in_specs=[pl.BlockSpec((tm, d), lambda i, pos: (i, 0)), pl.BlockSpec(memory_space=pl.ANY)],
        out_specs=pl.BlockSpec(memory_space=pl.ANY),
        scratch_shapes=[pltpu.VMEM((2, tm, d // 2), jnp.int32), pltpu.SemaphoreType.DMA((2,))])
    return pl.pallas_call(
        functools.partial(_moe_dispatch_kernel, tm=tm, n_tok=t, n_steps=t // tm), grid_spec=grid_spec,
        out_shape=jax.ShapeDtypeStruct((n_rows, d // 2), jnp.int32),
        input_output_aliases={2: 0},
        compiler_params=_cparams("arbitrary"), name="moe_dispatch")(pos_flat, xt, jnp.zeros((n_rows, d // 2), jnp.int32))


def _moe_ffn_kernel(be_ref, nb_ref, r_ref, wg_ref, wu_ref, wd_ref, o_ref):
    del be_ref
    i = pl.program_id(0)

    @pl.when(i < nb_ref[0])
    def _():
        r = _unpack_halves(r_ref[...]).astype(BF16)
        hidden = _silu(_dot(r, wg_ref[...].astype(BF16))) * _dot(r, wu_ref[...].astype(BF16))
        o_ref[...] = _pack_halves(_dot(hidden.astype(BF16), wd_ref[...].astype(BF16)))

    @pl.when(i >= nb_ref[0])
    def _():
        o_ref[...] = jnp.zeros(o_ref.shape, o_ref.dtype)


def _moe_ffn(xs, block_expert, n_used, w_gate, w_up, w_down, tm):
    n_rows, half = xs.shape
    d = 2 * half
    ff = w_gate.shape[2]
    grid_spec = pltpu.PrefetchScalarGridSpec(
        num_scalar_prefetch=2, grid=(n_rows // tm,),
        in_specs=[pl.BlockSpec((tm, half), lambda i, be, nb: (i, 0)),
                  pl.BlockSpec((None, d, ff), lambda i, be, nb: (be[i], 0, 0)),
                  pl.BlockSpec((None, d, ff), lambda i, be, nb: (be[i], 0, 0)),
                  pl.BlockSpec((None, ff, d), lambda i, be, nb: (be[i], 0, 0))],
        out_specs=pl.BlockSpec((tm, half), lambda i, be, nb: (i, 0)))
    return pl.pallas_call(
        _moe_ffn_kernel, grid_spec=grid_spec,
        out_shape=jax.ShapeDtypeStruct((n_rows, half), jnp.int32),
        compiler_params=_cparams("arbitrary"), name="moe_ffn")(block_expert, n_used, xs, w_gate, w_up, w_down)


def _combine_ln_kernel(pos_ref, rows_hbm, x_ref, gt_ref, g_ref, b_ref, o_ref, buf, sem, *, tm, n_tok, n_steps, alpha):
    i = pl.program_id(0)
    slot = i % 2

    def row_copy(src_row, sl, r):
        return pltpu.make_async_copy(rows_hbm.at[pl.ds(src_row, 1)], buf.at[sl, pl.ds(r, 1)], sem.at[sl])

    def start(step, sl):
        for kk in range(TOP_K):
            for r in range(tm):
                row_copy(pos_ref[kk * n_tok + step * tm + r], sl, kk * tm + r).start(priority=r % 2)

    @pl.when(i == 0)
    def _():
        start(0, 0)

    @pl.when(i + 1 < n_steps)
    def _():
        start(i + 1, 1 - slot)

    for r in range(TOP_K * tm):
        row_copy(0, slot, r).wait()
    gt = gt_ref[...]
    y = gt[:, 0:1] * _unpack_halves(buf[slot, 0:tm, :]) + gt[:, 1:2] * _unpack_halves(buf[slot, tm:2 * tm, :])
    r = alpha * x_ref[...] + y
    o_ref[...] = _layer_norm(r, g_ref[...], b_ref[...])


def _combine_ln(out_rows, pos_flat, xt, gates_tk, gain, bias, alpha, tm=128):
    t, d = xt.shape
    n_steps = t // tm
    grid_spec = pltpu.PrefetchScalarGridSpec(
        num_scalar_prefetch=1, grid=(n_steps,),
        in_specs=[pl.BlockSpec(memory_space=pl.ANY),
                  pl.BlockSpec((tm, d), lambda i, pos: (i, 0)),
                  pl.BlockSpec((tm, TOP_K), lambda i, pos: (i, 0)),
                  pl.BlockSpec((1, d), lambda i, pos: (0, 0)), pl.BlockSpec((1, d), lambda i, pos: (0, 0))],
        out_specs=pl.BlockSpec((tm, d), lambda i, pos: (i, 0)),
        scratch_shapes=[pltpu.VMEM((2, TOP_K * tm, d // 2), jnp.int32), pltpu.SemaphoreType.DMA((2,))])
    return pl.pallas_call(
        functools.partial(_combine_ln_kernel, tm=tm, n_tok=t, n_steps=n_steps, alpha=alpha), grid_spec=grid_spec,
        out_shape=jax.ShapeDtypeStruct((t, d), F32),
        compiler_params=_cparams("arbitrary"), name="combine_ln")(
            pos_flat, out_rows, xt, gates_tk, gain.reshape(1, d), bias.reshape(1, d))


MOE_TM = 256


def _moe_layer(xt, router_wt, bias_col, w_gate, w_up, w_down, gain, bias, alpha):
    t, d = xt.shape
    n_pairs = t * TOP_K
    n_blocks = n_pairs // MOE_TM + N_EXPERTS
    idx, gates = _router(xt, router_wt, bias_col)
    experts = jnp.arange(N_EXPERTS, dtype=jnp.int32)
    counts = jnp.sum((idx.reshape(n_pairs, 1) == experts[None, :]).astype(jnp.int32), axis=0)
    padded = (counts + MOE_TM - 1) // MOE_TM * MOE_TM
    padded_end = jnp.cumsum(padded)
    block_row0 = jnp.arange(n_blocks, dtype=jnp.int32) * MOE_TM
    block_expert = jnp.minimum(jnp.sum((padded_end[None, :] <= block_row0[:, None]).astype(jnp.int32), axis=1),
                               N_EXPERTS - 1).astype(jnp.int32)
    n_used = (padded_end[-1:] // MOE_TM).astype(jnp.int32)
    pos = _moe_rank(idx, (padded_end - padded).astype(F32).reshape(N_EXPERTS, 1))
    pos_flat = pos.reshape(n_pairs)
    xs = _moe_dispatch(xt, pos_flat, n_blocks * MOE_TM)
    out_rows = _moe_ffn(xs, block_expert, n_used, w_gate, w_up, w_down, MOE_TM)
    return _combine_ln(out_rows, pos_flat, xt, gates.T, gain, bias, alpha)


def _delta_fox_layer(xt, batch, seq, w_in, conv_w, a_log, dt_bias, norm_w, f_bias, w_out, gain, bias, alpha):
    o_qkv, o_z, o_beta, o_a, o_fox, o_f = 0, 1536, 2048, 2056, 2064, 3600
    zeros = lambda n: jnp.zeros((w_in.shape[0], n), w_in.dtype)
    w_small = jnp.concatenate([w_in[:, o_f:o_f + 8], w_in[:, o_beta:o_beta + 8], w_in[:, o_a:o_a + 8], zeros(LANES - 24)], axis=1)
    ws = [w_in[:, o_qkv:o_z], w_in[:, o_z:o_beta], w_small, w_in[:, o_fox:o_fox + FOX_WIDTH], w_in[:, o_fox + FOX_WIDTH:o_f]]
    qkv_pre, z, small, fox_q, fox_kv = _proj(
        xt, [w.astype(BF16) for w in ws], [F32, F32, F32, BF16, BF16], tm=256,
        out_scales=(1.0, 1.0, 1.0, FOX_D ** -0.5 * LOG2E, 1.0))

    def lane_row(v, off):
        return jnp.zeros((1, LANES), F32).at[0, off:off + v.shape[0]].set(v.astype(F32))

    qe, w, u, kt, ol, dl = _gdn_local(qkv_pre, small, conv_w.astype(F32), lane_row(a_log, 16), lane_row(dt_bias, 16), batch, seq)
    o_d = _gdn_scan(qe, w, u, kt, ol, dl, z, jnp.tile(norm_w.astype(F32), GDN_HEADS).reshape(1, GDN_WIDTH), batch, seq, BF16)
    cq, ck = _fox_gate(small, lane_row(f_bias, 0), batch, seq)
    ck = ck.reshape(batch, FOX_HEADS // 2, 2, seq)
    o_f = _attention(fox_q, 0, fox_kv, 0, fox_kv, FOX_WIDTH // LANES, FOX_HEADS, batch, seq, True, BF16, cq=cq, ck=ck)
    w_out = w_out.astype(BF16)
    return _out_ln([o_d, o_f], [w_out[:GDN_WIDTH], w_out[GDN_WIDTH:]], xt, gain, bias, alpha, tm=256)


def _mla_layer(xt, positions, batch, seq, w_in, q_norm, kv_norm, w_uq, w_ukv, w_out, gain, bias, alpha):
    t = xt.shape[0]
    half = MLA_ROPE // 2
    o_kv, o_kr = MLA_Q_LORA, MLA_Q_LORA + MLA_KV_LORA

    def rope_cols(w):
        return jnp.concatenate([w, -w[:, half:], w[:, :half]], axis=1)

    w_kr = jnp.concatenate([jnp.zeros((w_in.shape[0], MLA_NOPE), w_in.dtype), rope_cols(w_in[:, o_kr:])], axis=1)
    ws = [w_in[:, :o_kv], w_in[:, o_kv:o_kr], w_kr]
    c_q, c_kv, kr_blk = _proj(xt, [w.astype(BF16) for w in ws], [F32, F32, F32], tm=256)

    dq = MLA_NOPE + MLA_ROPE
    wq3 = w_uq.reshape(MLA_Q_LORA, MLA_HEADS, dq)
    wq_blk = jnp.concatenate([wq3, -wq3[:, :, MLA_NOPE + half:], wq3[:, :, MLA_NOPE:MLA_NOPE + half]], axis=2)
    wq_blk = wq_blk.reshape(MLA_Q_LORA, MLA_HEADS * LANES).astype(BF16)
    wkv3 = w_ukv.reshape(MLA_KV_LORA, MLA_HEADS, MLA_NOPE + MLA_V)
    wk_blk = jnp.concatenate([wkv3[:, :, :MLA_NOPE], jnp.zeros((MLA_KV_LORA, MLA_HEADS, LANES - MLA_NOPE), w_ukv.dtype)], axis=2)
    wk_blk = wk_blk.reshape(MLA_KV_LORA, MLA_HEADS * LANES).astype(BF16)
    wv_blk = wkv3[:, :, MLA_NOPE:].reshape(MLA_KV_LORA, MLA_HEADS * MLA_V).astype(BF16)

    inv_freq = ROPE_THETA ** (-jnp.arange(0, MLA_ROPE, 2, dtype=F32) / MLA_ROPE)
    inv_row = jnp.concatenate([jnp.zeros((MLA_NOPE,), F32)] + [inv_freq] * 4).reshape(1, LANES)
    cos, sin = _rope_tables(positions.reshape(t, 1).astype(jnp.int32), inv_row)

    q_blk = _mla_q(c_q, q_norm.astype(F32).reshape(1, -1), wq_blk, cos, sin)
    k_blk, v_blk = _mla_kv(c_kv, kv_norm.astype(F32).reshape(1, -1), kr_blk, wk_blk, wv_blk, cos, sin)
    o = _attention(q_blk, 0, k_blk, 0, v_blk, 0, MLA_HEADS, batch, seq, False, BF16)
    return _out_ln([o], [w_out.astype(BF16)], xt, gain, bias, alpha, tm=256)


def kernel(x, positions, ln_gain, ln_bias, router_w, router_bias, moe_w_gate, moe_w_up, moe_w_down, hy_w_in, hy_conv_w,
           gdn_a_log, gdn_dt_bias, gdn_norm_w, fox_f_bias, hy_w_out, mla_w_in, mla_q_norm, mla_kv_norm, mla_w_uq,
           mla_w_ukv, mla_w_out):
    batch, seq, d = x.shape
    depth = ln_gain.shape[0]
    alpha = (2.0 * depth) ** 0.25
    xt = x.reshape(batch * seq, d)
    router_wt = router_w.astype(F32).T
    bias_col = router_bias.astype(F32).reshape(N_EXPERTS, 1)
    for layer in range(depth):
        j = layer // 2
        if layer % 2 == 0:
            xt = _delta_fox_layer(xt, batch, seq, hy_w_in[j], hy_conv_w[j], gdn_a_log[j], gdn_dt_bias[j], gdn_norm_w[j],
                                  fox_f_bias[j], hy_w_out[j], ln_gain[layer, 0], ln_bias[layer, 0], alpha)
        else:
            xt = _mla_layer(xt, positions, batch, seq, mla_w_in[j], mla_q_norm[j], mla_kv_norm[j], mla_w_uq[j],
                            mla_w_ukv[j], mla_w_out[j], ln_gain[layer, 0], ln_bias[layer, 0], alpha)
        xt = _moe_layer(xt, router_wt, bias_col, moe_w_gate[layer], moe_w_up[layer], moe_w_down[layer],
                        ln_gain[layer, 1], ln_bias[layer, 1], alpha)
    return xt.reshape(batch, seq, d)
```

```python
import functools
import math

import numpy as np
import jax
import jax.numpy as jnp
from jax import lax
from jax.experimental import pallas as pl
from jax.experimental.pallas import tpu as pltpu

F32 = jnp.float32
BF16 = jnp.bfloat16

D_MODEL = 1024
LN_EPS = 1e-5
RMS_EPS = 1e-6
NEG_INF = -1e30
LANES = 128

GDN_HEADS = 8
GDN_D = 64
GDN_CHUNK = 64
CONV_K = 4
GDN_WIDTH = GDN_HEADS * GDN_D
FOX_HEADS = 8
FOX_D = 64
FOX_WIDTH = FOX_HEADS * FOX_D
MLA_HEADS = 16
MLA_Q_LORA = 512
MLA_KV_LORA = 256
MLA_NOPE = 64
MLA_ROPE = 32
MLA_V = 64
ROPE_THETA = 10000.0
N_EXPERTS = 64
N_GROUPS = 8
EXPERTS_PER_GROUP = 8
TOP_K = 2
D_FF_EXPERT = 256

VMEM_LIMIT = 56 * 1024 * 1024
HIGHEST = lax.Precision.HIGHEST
LOG2E = 1.4426950408889634


def _cparams(*sem, flags=None):
    return pltpu.CompilerParams(dimension_semantics=sem, vmem_limit_bytes=VMEM_LIMIT, flags=flags)


def _dot(a, b, precision=None):
    return jnp.dot(a, b, preferred_element_type=F32, precision=precision)


def _dot_nt(a, b, precision=None):
    return lax.dot_general(a, b, (((1,), (1,)), ((), ())), preferred_element_type=F32, precision=precision)


def _dot_tn(a, b, precision=None):
    return lax.dot_general(a, b, (((0,), (0,)), ((), ())), preferred_element_type=F32, precision=precision)


def _split_bf16(x):
    hi = x.astype(BF16)
    return hi, (x - hi.astype(F32)).astype(BF16)


def _dot3(a, b):
    a_hi, a_lo = _split_bf16(a)
    b_hi, b_lo = _split_bf16(b)
    return (_dot(jnp.concatenate([a_hi, a_lo], axis=1), jnp.concatenate([b_hi, b_hi], axis=0))
            + _dot(a_hi, b_lo))


def _pack_halves(y):
    n = y.shape[1] // 2
    lo = lax.bitcast_convert_type(y[:, :n].astype(BF16).astype(F32), jnp.int32)
    hi = lax.bitcast_convert_type(y[:, n:].astype(BF16).astype(F32), jnp.int32)
    return hi | lax.shift_right_logical(lo, 16)


def _unpack_halves(packed):
    lo = lax.bitcast_convert_type(lax.shift_left(packed, 16), F32)
    hi = lax.bitcast_convert_type(packed & jnp.int32(-65536), F32)
    return jnp.concatenate([lo, hi], axis=1)


def _silu(x):
    return x * jax.nn.sigmoid(x)


def _layer_norm(r, gain, bias):
    mu = jnp.mean(r, axis=-1, keepdims=True)
    d = r - mu
    var = jnp.mean(d * d, axis=-1, keepdims=True)
    return d * lax.rsqrt(var + LN_EPS) * gain + bias


def _rms_norm(x, gain):
    return x * lax.rsqrt(jnp.mean(x * x, axis=-1, keepdims=True) + RMS_EPS) * gain


def _proj_kernel(x_ref, *refs, out_scales):
    n = len(refs) // 2
    x = x_ref[...].astype(BF16)
    for w_ref, o_ref, scale in zip(refs[:n], refs[n:], out_scales):
        y = _dot(x, w_ref[...])
        o_ref[...] = (y if scale == 1.0 else y * scale).astype(o_ref.dtype)


def _proj(x, ws, out_dtypes, tm, out_scales=None):
    m, k = x.shape
    out_scales = tuple(out_scales) if out_scales is not None else (1.0,) * len(ws)
    in_specs = [pl.BlockSpec((tm, k), lambda i: (i, 0))]
    in_specs += [pl.BlockSpec(w.shape, lambda i: (0, 0)) for w in ws]
    out_specs = [pl.BlockSpec((tm, w.shape[1]), lambda i: (i, 0)) for w in ws]
    out_shape = [jax.ShapeDtypeStruct((m, w.shape[1]), dt) for w, dt in zip(ws, out_dtypes)]
    return pl.pallas_call(
        functools.partial(_proj_kernel, out_scales=out_scales), grid=(m // tm,), in_specs=in_specs, out_specs=out_specs, out_shape=out_shape,
        compiler_params=_cparams("parallel"), name="proj")(x, *ws)


def _out_ln_kernel(*refs, n_in, alpha):
    a_refs = refs[:n_in]
    w_refs = refs[n_in:2 * n_in]
    x_ref, g_ref, b_ref, o_ref = refs[2 * n_in:]
    y = None
    for a_ref, w_ref in zip(a_refs, w_refs):
        t = _dot(a_ref[...].astype(BF16), w_ref[...])
        y = t if y is None else y + t
    r = alpha * x_ref[...] + y
    o_ref[...] = _layer_norm(r, g_ref[...], b_ref[...])


def _out_ln(acts, ws, x, gain, bias, alpha, tm):
    m, d = x.shape
    n_in = len(acts)
    in_specs = [pl.BlockSpec((tm, a.shape[1]), lambda i: (i, 0)) for a in acts]
    in_specs += [pl.BlockSpec(w.shape, lambda i: (0, 0)) for w in ws]
    in_specs += [pl.BlockSpec((tm, d), lambda i: (i, 0)),
                 pl.BlockSpec((1, d), lambda i: (0, 0)), pl.BlockSpec((1, d), lambda i: (0, 0))]
    return pl.pallas_call(
        functools.partial(_out_ln_kernel, n_in=n_in, alpha=alpha),
        grid=(m // tm,), in_specs=in_specs, out_specs=pl.BlockSpec((tm, d), lambda i: (i, 0)),
        out_shape=jax.ShapeDtypeStruct((m, d), F32),
        compiler_params=_cparams("parallel"), name="out_ln")(*acts, *ws, x, gain.reshape(1, d), bias.reshape(1, d))


def _attn_kernel(*refs, tq, tkc, packed, has_bias):
    refs = list(refs)
    q_sc = refs.pop() if packed else None
    if has_bias:
        q_ref, k_ref, v_ref, cq_ref, ck_ref, o_ref, m_sc, acc_sc, cq_sc = refs
    else:
        q_ref, k_ref, v_ref, o_ref, m_sc, acc_sc = refs
    p_id = pl.program_id(1)
    i = pl.program_id(2)
    assert tq == 2 * tkc
    n_below = 2 * i

    m_sc[...] = jnp.full(m_sc.shape, NEG_INF, F32)
    acc_sc[...] = jnp.zeros(acc_sc.shape, F32)
    lane_q = lax.broadcasted_iota(jnp.int32, (tq, LANES), 1)
    if has_bias:
        cq = cq_ref[...]
        for hh in range(2):
            col = jnp.sum(jnp.where(lane_q == 2 * p_id + hh, cq, 0.0), axis=1, keepdims=True)
            cq_sc[hh] = jnp.broadcast_to(col, (tq, LANES))
    lane_v = lax.broadcasted_iota(jnp.int32, (tkc, LANES), 1)
    if packed:
        q2 = q_ref[...]
        for hh in range(2):
            q_sc[hh] = jnp.where((lane_q >= 64 * hh) & (lane_q < 64 * (hh + 1)), q2, jnp.zeros_like(q2))
    n_blk = tkc // LANES

    def chunk(c, masked):
        k0 = pl.multiple_of(c * tkc, tkc)
        v = v_ref[pl.ds(k0, tkc), :]
        ones = jnp.ones_like(v)
        v_aug = (jnp.where(lane_v < 64, v, ones), jnp.where(lane_v < 64, ones, v))
        if masked:
            row = i * tq + lax.broadcasted_iota(jnp.int32, (tq, LANES), 0)
            col = k0 + lax.broadcasted_iota(jnp.int32, (tq, LANES), 1)

        def qk(hh):
            if packed:
                return _dot_nt(q_sc[hh], k_ref[pl.ds(k0, tkc), :])
            return _dot_nt(q_ref[:, hh * LANES:(hh + 1) * LANES], k_ref[pl.ds(k0, tkc), hh * LANES:(hh + 1) * LANES])

        logits = None if has_bias else [qk(0), qk(1)]
        for hh in range(2):
            s = qk(hh) if logits is None else logits[hh]
            if has_bias:
                ck_row = ck_ref[0, 0, hh:hh + 1, pl.ds(k0, tkc)]
            blocks = []
            for bi in range(n_blk):
                sb = s[:, bi * LANES:(bi + 1) * LANES]
                if has_bias:
                    sb = sb - ck_row[:, bi * LANES:(bi + 1) * LANES]
                if masked:
                    sb = jnp.where(col + bi * LANES <= row, sb, NEG_INF)
                blocks.append(sb)
            m_cur = functools.reduce(jnp.maximum, blocks)
            m_cur = jnp.broadcast_to(jnp.max(m_cur, axis=1, keepdims=True), (tq, LANES))
            if has_bias:
                m_cur = m_cur + cq_sc[hh]
            m_prev = m_sc[hh]
            m_new = jnp.maximum(m_prev, m_cur)
            alpha = jnp.exp2(m_prev - m_new)
            shift = m_new - cq_sc[hh] if has_bias else m_new
            if has_bias:
                p = jnp.concatenate([jnp.exp2((sb - shift).astype(BF16)) for sb in blocks], axis=1)
            else:
                p = jnp.concatenate([jnp.exp2(sb - shift).astype(BF16) for sb in blocks], axis=1)
            acc_sc[hh] = alpha * acc_sc[hh] + _dot(p, v_aug[hh])
            m_sc[hh] = m_new

    def below_quad(j, carry):
        for u in range(4):
            chunk(4 * j + u, False)
        return carry

    lax.fori_loop(0, n_below // 4, below_quad, 0)

    @pl.when(n_below % 4 == 2)
    def _():
        chunk(n_below - 2, False)
        chunk(n_below - 1, False)

    chunk(n_below, True)
    chunk(n_below + 1, True)

    acc0 = acc_sc[0]
    acc1 = acc_sc[1]
    o0 = acc0 / pltpu.roll(acc0, 64, 1)
    o1 = acc1 / pltpu.roll(acc1, 64, 1)
    o_ref[...] = jnp.where(lane_q < 64, o0, o1).astype(o_ref.dtype)


def _attention(q_arr, q_off, k_arr, k_off, v_arr, v_off, n_heads, batch, seq, packed, out_dtype,
               cq=None, ck=None, tq=1024, tkc=512, flags=None):
    t = batch * seq
    hp = n_heads // 2
    qk_w = LANES if packed else 2 * LANES
    nq = seq // tq
    has_bias = cq is not None
    in_specs = [pl.BlockSpec((tq, qk_w), lambda b, p, i: (b * nq + i, q_off + p)),
                pl.BlockSpec((seq, qk_w), lambda b, p, i: (b, k_off + p)),
                pl.BlockSpec((seq, LANES), lambda b, p, i: (b, v_off + p))]
    args = [q_arr, k_arr, v_arr]
    scratch = [pltpu.VMEM((2, tq, LANES), F32), pltpu.VMEM((2, tq, LANES), F32)]
    if has_bias:
        in_specs += [pl.BlockSpec((tq, LANES), lambda b, p, i: (b * nq + i, 0)),
                     pl.BlockSpec((1, 1, 2, seq), lambda b, p, i: (b, p, 0, 0))]
        args += [cq, ck]
        scratch += [pltpu.VMEM((2, tq, LANES), F32)]
    if packed:
        scratch += [pltpu.VMEM((2, tq, LANES), q_arr.dtype)]
    return pl.pallas_call(
        functools.partial(_attn_kernel, tq=tq, tkc=tkc, packed=packed, has_bias=has_bias),
        grid=(batch, hp, nq), in_specs=in_specs,
        out_specs=pl.BlockSpec((tq, LANES), lambda b, p, i: (b * nq + i, p)),
        out_shape=jax.ShapeDtypeStruct((t, hp * LANES), out_dtype),
        scratch_shapes=scratch,
        compiler_params=_cparams("parallel", "parallel", "arbitrary", flags=flags), name="attention")(*args)


def _fox_gate_kernel(s_ref, fb_ref, cq_ref, ck_ref, carry, *, tm):
    @pl.when(pl.program_id(1) == 0)
    def _():
        carry[...] = jnp.zeros(carry.shape, F32)

    logf = jax.nn.log_sigmoid(s_ref[...] + fb_ref[...])
    row = lax.broadcasted_iota(jnp.int32, (tm, tm), 0)
    col = lax.broadcasted_iota(jnp.int32, (tm, tm), 1)
    tri = (col <= row).astype(F32)
    c = _dot(tri, logf, HIGHEST) + carry[0:1, :]
    carry[...] = jnp.broadcast_to(c[tm - 1:tm, :], carry.shape)
    c2 = c * LOG2E
    cq_ref[...] = c2
    ck_ref[0] = c2.T[0:8, :]


def _fox_gate(small, f_bias_row, batch, seq, tm=256):
    t = batch * seq
    n = seq // tm
    return pl.pallas_call(
        functools.partial(_fox_gate_kernel, tm=tm), grid=(batch, n),
        in_specs=[pl.BlockSpec((tm, LANES), lambda b, i: (b * n + i, 0)), pl.BlockSpec((1, LANES), lambda b, i: (0, 0))],
        out_specs=[pl.BlockSpec((tm, LANES), lambda b, i: (b * n + i, 0)), pl.BlockSpec((1, 8, tm), lambda b, i: (b, 0, i))],
        out_shape=[jax.ShapeDtypeStruct((t, LANES), F32), jax.ShapeDtypeStruct((batch, 8, seq), F32)],
        scratch_shapes=[pltpu.VMEM((8, LANES), F32)],
        compiler_params=_cparams("parallel", "arbitrary"), name="fox_gate")(small, f_bias_row)


CHUNKS_PER_STEP = 2


def _gdn_local_kernel(x_ref, halo_ref, s_ref, cw_ref, alog_ref, dtb_ref,
                      qe_ref, w_ref, u_ref, kt_ref, ol_ref, dl_ref,
                      y_sc, beta_sc, ld_sc, *, tm, tiles_per_batch):
    c_sz = GDN_CHUNK
    n_chunks = tm // c_sz
    first = (pl.program_id(0) % tiles_per_batch) == 0
    halo = jnp.where(first, 0.0, halo_ref[...])
    ext = jnp.concatenate([halo, x_ref[...]], axis=0)
    acc = None
    for jj in range(CONV_K):
        term = cw_ref[jj:jj + 1, :] * ext[8 - (CONV_K - 1) + jj: 8 - (CONV_K - 1) + jj + tm, :]
        acc = term if acc is None else acc + term
    y_sc[...] = _silu(acc)
    sm = s_ref[...]
    beta_sc[...] = jax.nn.sigmoid(sm)
    ld_sc[...] = -jnp.exp(alog_ref[...]) * jax.nn.softplus(sm + dtb_ref[...])

    r64 = lax.broadcasted_iota(jnp.int32, (c_sz, c_sz), 0)
    c64 = lax.broadcasted_iota(jnp.int32, (c_sz, c_sz), 1)
    tri64 = (c64 <= r64).astype(F32)
    lane64 = lax.broadcasted_iota(jnp.int32, (c_sz, LANES), 1) < 64
    rowp = lax.broadcasted_iota(jnp.int32, (LANES, LANES), 0)
    colp = lax.broadcasted_iota(jnp.int32, (LANES, LANES), 1)
    row_h1 = rowp >= 64
    same_head = row_h1 == (colp >= 64)
    ti = rowp % 64
    tj = colp % 64
    incl = same_head & (tj <= ti)
    strict = same_head & (tj < ti)
    eye = (rowp == colp).astype(F32)
    lane_h1_full = lax.broadcasted_iota(jnp.int32, (LANES, LANES), 1) >= 64
    scale = GDN_D ** -0.5

    def sel(a0, a1):
        return jnp.where(lane64, a0, a1)

    n_pairs = GDN_HEADS // 2

    def prepare(c, prep):
        r0 = pl.multiple_of(c * c_sz, c_sz)
        rows = pl.ds(r0, c_sz)
        bl = beta_sc[rows, :]
        g = _dot(tri64, ld_sc[rows, :], HIGHEST)
        g_t = jnp.concatenate([g, g], axis=0).T
        dl_row = []
        for p in range(n_pairs):
            h0, h1 = 2 * p, 2 * p + 1
            q2 = y_sc[rows, p * LANES:(p + 1) * LANES]
            k2 = y_sc[rows, GDN_WIDTH + p * LANES:GDN_WIDTH + (p + 1) * LANES]
            v2 = y_sc[rows, 2 * GDN_WIDTH + p * LANES:2 * GDN_WIDTH + (p + 1) * LANES]

            def l2n(x2):
                sq = x2 * x2
                s0 = jnp.sum(jnp.where(lane64, sq, 0.0), axis=1, keepdims=True)
                s1 = jnp.sum(jnp.where(lane64, 0.0, sq), axis=1, keepdims=True)
                return x2 * sel(lax.rsqrt(s0 + RMS_EPS), lax.rsqrt(s1 + RMS_EPS))

            kn2 = l2n(k2)
            qs2 = l2n(q2) * scale
            bcol2 = sel(bl[:, 8 + h0:9 + h0], bl[:, 8 + h1:9 + h1])
            g0c = g[:, 16 + h0:17 + h0]
            g1c = g[:, 16 + h1:17 + h1]
            gcol2 = sel(g0c, g1c)
            eg2 = jnp.exp(gcol2)
            kb2 = kn2 * bcol2
            vb2 = v2 * bcol2
            kbg2 = kb2 * eg2
            qdec2 = qs2 * eg2
            gl0 = g[c_sz - 1:c_sz, 16 + h0:17 + h0]
            gl1 = g[c_sz - 1:c_sz, 16 + h1:17 + h1]
            glast2 = sel(gl0, gl1)
            ktail2 = kn2 * jnp.exp(glast2 - gcol2)
            dl_row.append(jnp.exp(glast2[0:1, :]))

            gcol_p = jnp.concatenate([jnp.broadcast_to(g0c, (c_sz, LANES)), jnp.broadcast_to(g1c, (c_sz, LANES))], axis=0)
            grow_p = jnp.where(row_h1, g_t[16 + h1:17 + h1, :], g_t[16 + h0:17 + h0, :])
            gdiff = gcol_p - grow_p
            decay = jnp.where(incl, jnp.exp(jnp.where(incl, gdiff, 0.0)), 0.0)

            kk = jnp.concatenate([kn2, kn2], axis=0).astype(BF16)
            kbm = jnp.concatenate([jnp.where(lane64, kb2, 0.0), jnp.where(lane64, 0.0, kb2)], axis=0).astype(BF16)
            qm = jnp.concatenate([jnp.where(lane64, qs2, 0.0), jnp.where(lane64, 0.0, qs2)], axis=0).astype(BF16)
            kq = _dot_nt(jnp.concatenate([kbm, qm], axis=0), kk)
            lower = jnp.where(strict, kq[:LANES] * decay, 0.0)
            attn = jnp.where(incl, kq[LANES:] * decay, 0.0)
            rhs = jnp.concatenate([jnp.concatenate([vb2, vb2], axis=0), jnp.concatenate([kbg2, kbg2], axis=0)], axis=1)
            prep.append((qdec2, ktail2, attn.astype(BF16), rhs.astype(BF16), -lower, rows, p))
        dl_ref[pl.ds(c, 1), :] = jnp.concatenate(dl_row, axis=1)

    def group_body(j, carry):
        prep = []
        for cc in range(CHUNKS_PER_STEP):
            prepare(j * CHUNKS_PER_STEP + cc, prep)

        mpow = [pr[4] for pr in prep]
        tinv = [eye + m for m in mpow]
        mpow = [_dot3(m, m) for m in mpow]
        for level in range(4):
            both = [_dot3(m, jnp.concatenate([m, t], axis=1)) for m, t in zip(mpow, tinv)]
            mpow = [bo[:, :LANES] for bo in both]
            tinv = [t + bo[:, LANES:] for t, bo in zip(tinv, both)]
        tinv = [t + _dot3(m, t) for m, t in zip(mpow, tinv)]
        uws = [_dot(t.astype(BF16), pr[3]) for t, pr in zip(tinv, prep)]
        olqs = [_dot(pr[2], uw.astype(BF16)) for uw, pr in zip(uws, prep)]
        for pr, uw, olq in zip(prep, uws, olqs):
            qdec2, ktail2, rows, p = pr[0], pr[1], pr[5], pr[6]
            u2 = sel(uw[:c_sz, :LANES], uw[c_sz:, :LANES])
            w2 = sel(uw[:c_sz, LANES:], uw[c_sz:, LANES:])
            ol2 = sel(olq[:c_sz, :LANES], olq[c_sz:, :LANES])
            aw2 = sel(olq[:c_sz, LANES:], olq[c_sz:, LANES:])
            cols = slice(p * LANES, (p + 1) * LANES)
            qe_ref[rows, cols] = qdec2 - aw2
            w_ref[rows, cols] = w2
            u_ref[rows, cols] = u2
            kt_ref[rows, cols] = ktail2
            ol_ref[rows, cols] = ol2
        return carry

    lax.fori_loop(0, n_chunks // CHUNKS_PER_STEP, group_body, 0)


def _gdn_local(qkv_pre, small, conv_w, alog_row, dtb_row, batch, seq, tm=512):
    t = batch * seq
    w3 = 3 * GDN_WIDTH
    tiles_per_batch = seq // tm
    tok = lambda i: (i, 0)
    outs = [jax.ShapeDtypeStruct((t, GDN_WIDTH), F32)] * 5 + [jax.ShapeDtypeStruct((t // GDN_CHUNK, GDN_WIDTH), F32)]
    out_specs = [pl.BlockSpec((tm, GDN_WIDTH), tok)] * 5 + [pl.BlockSpec((tm // GDN_CHUNK, GDN_WIDTH), tok)]
    return pl.pallas_call(
        functools.partial(_gdn_local_kernel, tm=tm, tiles_per_batch=tiles_per_batch),
        grid=(t // tm,),
        in_specs=[pl.BlockSpec((tm, w3), tok),
                  pl.BlockSpec((8, w3), lambda i: (jnp.maximum(i * (tm // 8) - 1, 0), 0)),
                  pl.BlockSpec((tm, LANES), tok),
                  pl.BlockSpec((CONV_K, w3), lambda i: (0, 0)),
                  pl.BlockSpec((1, LANES), lambda i: (0, 0)), pl.BlockSpec((1, LANES), lambda i: (0, 0))],
        out_specs=out_specs, out_shape=outs,
        scratch_shapes=[pltpu.VMEM((tm, w3), F32), pltpu.VMEM((tm, LANES), F32), pltpu.VMEM((tm, LANES), F32)],
        compiler_params=_cparams("parallel"), name="gdn_local")(qkv_pre, qkv_pre, small, conv_w, alog_row, dtb_row)


def _gdn_scan_kernel(qe_ref, w_ref, u_ref, kt_ref, ol_ref, dl_ref, z_ref, nw_ref, o_ref, s_sc, o_sc, *, tm, batch):
    c_sz = GDN_CHUNK
    n_chunks = tm // c_sz
    gw = 4 * GDN_D

    @pl.when(pl.program_id(0) == 0)
    def _():
        s_sc[...] = jnp.zeros(s_sc.shape, F32)

    rg = lax.broadcasted_iota(jnp.int32, (gw, gw), 0) // GDN_D
    cg = lax.broadcasted_iota(jnp.int32, (gw, gw), 1) // GDN_D
    bd_mask = rg == cg

    def chunk_body(c, carry):
        r0 = pl.multiple_of(c * c_sz, c_sz)
        rows = pl.ds(r0, c_sz)
        for b in range(batch):
            dl = dl_ref[b, pl.ds(c, 1), :]
            for gi in range(GDN_HEADS // 4):
                cols = slice(gi * gw, (gi + 1) * gw)
                s = s_sc[b, gi]
                lhs = jnp.concatenate([qe_ref[b, rows, cols], w_ref[b, rows, cols]], axis=0).astype(BF16)
                x = _dot(lhs, s.astype(BF16))
                o_sc[b, rows, cols] = x[:c_sz] + ol_ref[b, rows, cols]
                v_new = u_ref[b, rows, cols] - x[c_sz:]
                upd = _dot_tn(kt_ref[b, rows, cols].astype(BF16), v_new.astype(BF16))
                s_sc[b, gi] = s * dl[:, cols] + jnp.where(bd_mask, upd, 0.0)
        return carry

    lax.fori_loop(0, n_chunks, chunk_body, 0)

    w5 = GDN_WIDTH
    rh = lax.broadcasted_iota(jnp.int32, (w5, w5), 0) // GDN_D
    ch = lax.broadcasted_iota(jnp.int32, (w5, w5), 1) // GDN_D
    ones_bd = (rh == ch).astype(BF16)
    for b in range(batch):
        o = o_sc[b]
        sq = o * o
        hi = sq.astype(BF16)
        mid = (sq - hi.astype(F32)).astype(BF16)
        lo = (sq - hi.astype(F32) - mid.astype(F32)).astype(BF16)
        ms = (_dot(hi, ones_bd) + _dot(mid, ones_bd) + _dot(lo, ones_bd)) * (1.0 / GDN_D)
        o_ref[b] = (o * lax.rsqrt(ms + RMS_EPS) * nw_ref[...] * _silu(z_ref[b])).astype(o_ref.dtype)


def _gdn_scan(qe, w, u, kt, ol, dl, z, nw_row, batch, seq, out_dtype, tm=512):
    r3 = lambda a: a.reshape(batch, seq, GDN_WIDTH)
    blk = pl.BlockSpec((batch, tm, GDN_WIDTH), lambda i: (0, i, 0))
    dl3 = dl.reshape(batch, seq // GDN_CHUNK, GDN_WIDTH)
    out = pl.pallas_call(
        functools.partial(_gdn_scan_kernel, tm=tm, batch=batch), grid=(seq // tm,),
        in_specs=[blk] * 5 + [pl.BlockSpec((batch, tm // GDN_CHUNK, GDN_WIDTH), lambda i: (0, i, 0)), blk,
                              pl.BlockSpec((1, GDN_WIDTH), lambda i: (0, 0))],
        out_specs=blk, out_shape=jax.ShapeDtypeStruct((batch, seq, GDN_WIDTH), out_dtype),
        scratch_shapes=[pltpu.VMEM((batch, GDN_HEADS // 4, 4 * GDN_D, 4 * GDN_D), F32),
                        pltpu.VMEM((batch, tm, GDN_WIDTH), F32)],
        compiler_params=_cparams("arbitrary"), name="gdn_scan")(r3(qe), r3(w), r3(u), r3(kt), r3(ol), dl3, r3(z), nw_row)
    return out.reshape(batch * seq, GDN_WIDTH)


def _rope_kernel(pos_ref, inv_ref, cos_ref, sin_ref):
    ang = pos_ref[...].astype(F32) * inv_ref[...]
    lane = lax.broadcasted_iota(jnp.int32, ang.shape, 1)
    cos_ref[...] = jnp.where(lane < MLA_NOPE, 1.0, jnp.where(lane < MLA_NOPE + MLA_ROPE, jnp.cos(ang), 0.0))
    sin_ref[...] = jnp.where(lane >= MLA_NOPE + MLA_ROPE, jnp.sin(ang), 0.0)


def _rope_tables(pos_col, inv_row, tm=512):
    t = pos_col.shape[0]
    blk = pl.BlockSpec((tm, LANES), lambda i: (i, 0))
    return pl.pallas_call(
        _rope_kernel, grid=(t // tm,),
        in_specs=[pl.BlockSpec((tm, 1), lambda i: (i, 0)), pl.BlockSpec((1, LANES), lambda i: (0, 0))],
        out_specs=[blk, blk], out_shape=[jax.ShapeDtypeStruct((t, LANES), F32)] * 2,
        compiler_params=_cparams("parallel"), name="rope_tables")(pos_col, inv_row)


def _rotary(blocks, cos, sin, n_heads):
    width = n_heads * LANES
    cos_t = jnp.concatenate([cos] * n_heads, axis=1) if n_heads > 1 else cos
    sin_t = jnp.concatenate([sin] * n_heads, axis=1) if n_heads > 1 else sin
    return blocks * cos_t + pltpu.roll(blocks * sin_t, width - MLA_ROPE, 1)


def _mla_q_kernel(cq_ref, g_ref, w_ref, cos_ref, sin_ref, o_ref):
    y = _rms_norm(cq_ref[...], g_ref[...])
    q = _dot(y.astype(BF16), w_ref[...])
    q = _rotary(q, cos_ref[...], sin_ref[...], MLA_HEADS)
    o_ref[...] = (q * ((MLA_NOPE + MLA_ROPE) ** -0.5 * LOG2E)).astype(o_ref.dtype)


def _mla_q(c_q, gain_row, w_q, cos, sin, tm=256):
    t = c_q.shape[0]
    width = MLA_HEADS * LANES
    tok = lambda i: (i, 0)
    fixed = lambda i: (0, 0)
    return pl.pallas_call(
        _mla_q_kernel, grid=(t // tm,),
        in_specs=[pl.BlockSpec((tm, MLA_Q_LORA), tok), pl.BlockSpec((1, MLA_Q_LORA), fixed),
                  pl.BlockSpec(w_q.shape, fixed), pl.BlockSpec((tm, LANES), tok), pl.BlockSpec((tm, LANES), tok)],
        out_specs=pl.BlockSpec((tm, width), tok), out_shape=jax.ShapeDtypeStruct((t, width), BF16),
        compiler_params=_cparams("parallel"), name="mla_q")(c_q, gain_row, w_q, cos, sin)


def _mla_kv_kernel(ckv_ref, g_ref, kr_ref, wk_ref, wv_ref, cos_ref, sin_ref, k_ref, v_ref):
    y = _rms_norm(ckv_ref[...], g_ref[...]).astype(BF16)
    kr = _rotary(kr_ref[...], cos_ref[...], sin_ref[...], 1)
    lane = lax.broadcasted_iota(jnp.int32, kr.shape, 1)
    kr = jnp.where((lane >= MLA_NOPE) & (lane < MLA_NOPE + MLA_ROPE), kr, 0.0)
    k = _dot(y, wk_ref[...]) + jnp.concatenate([kr] * MLA_HEADS, axis=1)
    k_ref[...] = k.astype(k_ref.dtype)
    v_ref[...] = _dot(y, wv_ref[...]).astype(v_ref.dtype)


def _mla_kv(c_kv, gain_row, kr_blk, w_k, w_v, cos, sin, tm=256):
    t = c_kv.shape[0]
    tok = lambda i: (i, 0)
    fixed = lambda i: (0, 0)
    return pl.pallas_call(
        _mla_kv_kernel, grid=(t // tm,),
        in_specs=[pl.BlockSpec((tm, MLA_KV_LORA), tok), pl.BlockSpec((1, MLA_KV_LORA), fixed),
                  pl.BlockSpec((tm, LANES), tok), pl.BlockSpec(w_k.shape, fixed), pl.BlockSpec(w_v.shape, fixed),
                  pl.BlockSpec((tm, LANES), tok), pl.BlockSpec((tm, LANES), tok)],
        out_specs=[pl.BlockSpec((tm, w_k.shape[1]), tok), pl.BlockSpec((tm, w_v.shape[1]), tok)],
        out_shape=[jax.ShapeDtypeStruct((t, w_k.shape[1]), BF16), jax.ShapeDtypeStruct((t, w_v.shape[1]), BF16)],
        compiler_params=_cparams("parallel"), name="mla_kv")(c_kv, gain_row, kr_blk, w_k, w_v, cos, sin)


def _router_kernel(x_ref, wt_ref, bias_ref, idx_ref, gate_ref, *, tm):
    w_hi, w_lo = _split_bf16(wt_ref[...])
    x_hi, x_lo = _split_bf16(x_ref[...])
    part = _dot_nt(jnp.concatenate([w_hi, w_lo], axis=0), x_hi)
    logits = part[:N_EXPERTS] + part[N_EXPERTS:] + _dot_nt(w_hi, x_lo)
    scores = jax.nn.sigmoid(logits)
    biased = scores + bias_ref[...]
    epg = EXPERTS_PER_GROUP
    iota = lax.broadcasted_iota(jnp.int32, (epg, tm), 0)
    best = None
    for gi in range(N_GROUPS):
        blk = biased[gi * epg:(gi + 1) * epg, :]
        raw = scores[gi * epg:(gi + 1) * epg, :]
        m1 = jnp.max(blk, axis=0, keepdims=True)
        i1 = jnp.min(jnp.where(blk == m1, iota, epg), axis=0, keepdims=True)
        blk2 = jnp.where(iota == i1, -jnp.inf, blk)
        m2 = jnp.max(blk2, axis=0, keepdims=True)
        i2 = jnp.min(jnp.where(blk2 == m2, iota, epg), axis=0, keepdims=True)
        s1 = jnp.sum(jnp.where(iota == i1, raw, 0.0), axis=0, keepdims=True)
        s2 = jnp.sum(jnp.where(iota == i2, raw, 0.0), axis=0, keepdims=True)
        cand = (m1 + m2, gi * epg + i1, gi * epg + i2, s1, s2)
        if best is None:
            best = cand
        else:
            better = cand[0] > best[0]
            best = tuple(jnp.where(better, c, b) for c, b in zip(cand, best))
    _, e1, e2, s1, s2 = best
    denom = s1 + s2
    idx_ref[...] = jnp.concatenate([e1, e2], axis=0)
    gate_ref[...] = jnp.concatenate([s1 / denom, s2 / denom], axis=0)


def _router(xt, router_wt, bias_col, tm=512):
    t, d = xt.shape
    return pl.pallas_call(
        functools.partial(_router_kernel, tm=tm), grid=(t // tm,),
        in_specs=[pl.BlockSpec((tm, d), lambda i: (i, 0)), pl.BlockSpec((N_EXPERTS, d), lambda i: (0, 0)),
                  pl.BlockSpec((N_EXPERTS, 1), lambda i: (0, 0))],
        out_specs=[pl.BlockSpec((TOP_K, tm), lambda i: (0, i)), pl.BlockSpec((TOP_K, tm), lambda i: (0, i))],
        out_shape=[jax.ShapeDtypeStruct((TOP_K, t), jnp.int32), jax.ShapeDtypeStruct((TOP_K, t), F32)],
        compiler_params=_cparams("parallel"), name="router")(xt, router_wt, bias_col)


def _moe_rank_kernel(idx_ref, start_ref, pos_ref, carry, *, tm):
    @pl.when(pl.program_id(0) == 0)
    def _():
        carry[...] = jnp.zeros(carry.shape, F32)

    idx = idx_ref[...]
    e_iota = lax.broadcasted_iota(jnp.int32, (N_EXPERTS, tm), 0)
    oh = [(e_iota == idx[kk:kk + 1, :]).astype(F32) for kk in range(TOP_K)]
    both = oh[0] + oh[1]
    r_io = lax.broadcasted_iota(jnp.int32, (tm, tm), 0)
    c_io = lax.broadcasted_iota(jnp.int32, (tm, tm), 1)
    before = (r_io < c_io).astype(BF16)
    base = _dot(both.astype(BF16), before) + carry[:, 0:1] + start_ref[...]
    pos = [jnp.sum(o * base, axis=0, keepdims=True) for o in oh]
    pos_ref[...] = jnp.concatenate(pos, axis=0).astype(jnp.int32)
    carry[...] = carry[...] + jnp.sum(both, axis=1, keepdims=True)


def _moe_rank(idx, start_col, tm=512):
    t = idx.shape[1]
    return pl.pallas_call(
        functools.partial(_moe_rank_kernel, tm=tm), grid=(t // tm,),
        in_specs=[pl.BlockSpec((TOP_K, tm), lambda i: (0, i)), pl.BlockSpec((N_EXPERTS, 1), lambda i: (0, 0))],
        out_specs=pl.BlockSpec((TOP_K, tm), lambda i: (0, i)),
        out_shape=jax.ShapeDtypeStruct((TOP_K, t), jnp.int32),
        scratch_shapes=[pltpu.VMEM((N_EXPERTS, LANES), F32)],
        compiler_params=_cparams("arbitrary"), name="moe_rank")(idx, start_col)


def _moe_dispatch_kernel(pos_ref, x_ref, zero_hbm, xs_hbm, stage, sem, *, tm, n_tok, n_steps):
    del zero_hbm
    i = pl.program_id(0)
    slot = i % 2
    base = i * tm

    def row_copy(sl, r, dst):
        return pltpu.make_async_copy(stage.at[sl, pl.ds(r, 1)], xs_hbm.at[pl.ds(dst, 1)], sem.at[sl])

    def drain(sl):
        for _ in range(TOP_K * tm):
            row_copy(sl, 0, 0).wait()

    @pl.when(i >= 2)
    def _():
        drain(slot)

    stage[slot] = _pack_halves(x_ref[...])
    for kk in range(TOP_K):
        for r in range(tm):
            row_copy(slot, r, pos_ref[kk * n_tok + base + r]).start(priority=r % 2)

    @pl.when(i == n_steps - 1)
    def _():
        if n_steps > 1:
            drain(1 - slot)
        drain(slot)


def _moe_dispatch(xt, pos_flat, n_rows, tm=128):
    t, d = xt.shape
    grid_spec = pltpu.PrefetchScalarGridSpec(
        num_scalar_prefetch=1, grid=(t // tm,),
        in_specs=[pl.BlockSpec((tm, d), lambda i, pos: (i, 0)), pl.BlockSpec(memory_space=pl.ANY)],
        out_specs=pl.BlockSpec(memory_space=pl.ANY),
        scratch_shapes=[pltpu.VMEM((2, tm, d // 2), jnp.int32), pltpu.SemaphoreType.DMA((2,))])
    return pl.pallas_call(
        functools.partial(_moe_dispatch_kernel, tm=tm, n_tok=t, n_steps=t // tm), grid_spec=grid_spec,
        out_shape=jax.ShapeDtypeStruct((n_rows, d // 2), jnp.int32),
        input_output_aliases={2: 0},
        compiler_params=_cparams("arbitrary"), name="moe_dispatch")(pos_flat, xt, jnp.zeros((n_rows, d // 2), jnp.int32))


def _moe_ffn_kernel(be_ref, nb_ref, r_ref, wg_ref, wu_ref, wd_ref, o_ref):
    del be_ref
    i = pl.program_id(0)

    @pl.when(i < nb_ref[0])
    def _():
        r = _unpack_halves(r_ref[...]).astype(BF16)
        hidden = _silu(_dot(r, wg_ref[...].astype(BF16))) * _dot(r, wu_ref[...].astype(BF16))
        o_ref[...] = _pack_halves(_dot(hidden.astype(BF16), wd_ref[...].astype(BF16)))

    @pl.when(i >= nb_ref[0])
    def _():
        o_ref[...] = jnp.zeros(o_ref.shape, o_ref.dtype)


def _moe_ffn(xs, block_expert, n_used, w_gate, w_up, w_down, layer, tm):
    n_rows, half = xs.shape
    d = 2 * half
    ff = w_gate.shape[3]
    grid_spec = pltpu.PrefetchScalarGridSpec(
        num_scalar_prefetch=2, grid=(n_rows // tm,),
        in_specs=[pl.BlockSpec((tm, half), lambda i, be, nb: (i, 0)),
                  pl.BlockSpec((None, None, d, ff), lambda i, be, nb: (layer, be[i], 0, 0)),
                  pl.BlockSpec((None, None, d, ff), lambda i, be, nb: (layer, be[i], 0, 0)),
                  pl.BlockSpec((None, None, ff, d), lambda i, be, nb: (layer, be[i], 0, 0))],
        out_specs=pl.BlockSpec((tm, half), lambda i, be, nb: (i, 0)))
    return pl.pallas_call(
        _moe_ffn_kernel, grid_spec=grid_spec,
        out_shape=jax.ShapeDtypeStruct((n_rows, half), jnp.int32),
        compiler_params=_cparams("arbitrary"), name="moe_ffn")(block_expert, n_used, xs, w_gate, w_up, w_down)


def _combine_ln_kernel(pos_ref, rows_hbm, x_ref, gt_ref, g_ref, b_ref, o_ref, buf, sem, *, tm, n_tok, n_steps, alpha):
    i = pl.program_id(0)
    slot = i % 2

    def row_copy(src_row, sl, r):
        return pltpu.make_async_copy(rows_hbm.at[pl.ds(src_row, 1)], buf.at[sl, pl.ds(r, 1)], sem.at[sl])

    def start(step, sl):
        for kk in range(TOP_K):
            for r in range(tm):
                row_copy(pos_ref[kk * n_tok + step * tm + r], sl, kk * tm + r).start(priority=r % 2)

    @pl.when(i == 0)
    def _():
        start(0, 0)

    @pl.when(i + 1 < n_steps)
    def _():
        start(i + 1, 1 - slot)

    for r in range(TOP_K * tm):
        row_copy(0, slot, r).wait()
    gt = gt_ref[...]
    y = gt[:, 0:1] * _unpack_halves(buf[slot, 0:tm, :]) + gt[:, 1:2] * _unpack_halves(buf[slot, tm:2 * tm, :])
    r = alpha * x_ref[...] + y
    o_ref[...] = _layer_norm(r, g_ref[...], b_ref[...])


def _combine_ln(out_rows, pos_flat, xt, gates_tk, gain, bias, alpha, tm=128):
    t, d = xt.shape
    n_steps = t // tm
    grid_spec = pltpu.PrefetchScalarGridSpec(
        num_scalar_prefetch=1, grid=(n_steps,),
        in_specs=[pl.BlockSpec(memory_space=pl.ANY),
                  pl.BlockSpec((tm, d), lambda i, pos: (i, 0)),
                  pl.BlockSpec((tm, TOP_K), lambda i, pos: (i, 0)),
                  pl.BlockSpec((1, d), lambda i, pos: (0, 0)), pl.BlockSpec((1, d), lambda i, pos: (0, 0))],
        out_specs=pl.BlockSpec((tm, d), lambda i, pos: (i, 0)),
        scratch_shapes=[pltpu.VMEM((2, TOP_K * tm, d // 2), jnp.int32), pltpu.SemaphoreType.DMA((2,))])
    return pl.pallas_call(
        functools.partial(_combine_ln_kernel, tm=tm, n_tok=t, n_steps=n_steps, alpha=alpha), grid_spec=grid_spec,
        out_shape=jax.ShapeDtypeStruct((t, d), F32),
        compiler_params=_cparams("arbitrary"), name="combine_ln")(
            pos_flat, out_rows, xt, gates_tk, gain.reshape(1, d), bias.reshape(1, d))


MOE_TM = 256


def _moe_layer(xt, router_wt, bias_col, w_gate, w_up, w_down, layer, gain, bias, alpha):
    t, d = xt.shape
    n_pairs = t * TOP_K
    n_blocks = n_pairs // MOE_TM + N_EXPERTS
    idx, gates = _router(xt, router_wt, bias_col)
    experts = jnp.arange(N_EXPERTS, dtype=jnp.int32)
    counts = jnp.sum((idx.reshape(n_pairs, 1) == experts[None, :]).astype(jnp.int32), axis=0)
    padded = (counts + MOE_TM - 1) // MOE_TM * MOE_TM
    padded_end = jnp.cumsum(padded)
    block_row0 = jnp.arange(n_blocks, dtype=jnp.int32) * MOE_TM
    block_expert = jnp.minimum(jnp.sum((padded_end[None, :] <= block_row0[:, None]).astype(jnp.int32), axis=1),
                               N_EXPERTS - 1).astype(jnp.int32)
    n_used = (padded_end[-1:] // MOE_TM).astype(jnp.int32)
    pos = _moe_rank(idx, (padded_end - padded).astype(F32).reshape(N_EXPERTS, 1))
    pos_flat = pos.reshape(n_pairs)
    xs = _moe_dispatch(xt, pos_flat, n_blocks * MOE_TM)
    out_rows = _moe_ffn(xs, block_expert, n_used, w_gate, w_up, w_down, layer, MOE_TM)
    return _combine_ln(out_rows, pos_flat, xt, gates.T, gain, bias, alpha)


def _delta_fox_layer(xt, batch, seq, w_in, conv_w, a_log, dt_bias, norm_w, f_bias, w_out, gain, bias, alpha):
    o_qkv, o_z, o_beta, o_a, o_fox, o_f = 0, 1536, 2048, 2056, 2064, 3600
    zeros = lambda n: jnp.zeros((w_in.shape[0], n), w_in.dtype)
    w_small = jnp.concatenate([w_in[:, o_f:o_f + 8], w_in[:, o_beta:o_beta + 8], w_in[:, o_a:o_a + 8], zeros(LANES - 24)], axis=1)
    ws = [w_in[:, o_qkv:o_z], w_in[:, o_z:o_beta], w_small, w_in[:, o_fox:o_fox + FOX_WIDTH], w_in[:, o_fox + FOX_WIDTH:o_f]]
    qkv_pre, z, small, fox_q, fox_kv = _proj(
        xt, [w.astype(BF16) for w in ws], [F32, F32, F32, BF16, BF16], tm=256,
        out_scales=(1.0, 1.0, 1.0, FOX_D ** -0.5 * LOG2E, 1.0))

    def lane_row(v, off):
        return jnp.zeros((1, LANES), F32).at[0, off:off + v.shape[0]].set(v.astype(F32))

    qe, w, u, kt, ol, dl = _gdn_local(qkv_pre, small, conv_w.astype(F32), lane_row(a_log, 16), lane_row(dt_bias, 16), batch, seq)
    o_d = _gdn_scan(qe, w, u, kt, ol, dl, z, jnp.tile(norm_w.astype(F32), GDN_HEADS).reshape(1, GDN_WIDTH), batch, seq, BF16)
    cq, ck = _fox_gate(small, lane_row(f_bias, 0), batch, seq)
    ck = ck.reshape(batch, FOX_HEADS // 2, 2, seq)
    o_f = _attention(fox_q, 0, fox_kv, 0, fox_kv, FOX_WIDTH // LANES, FOX_HEADS, batch, seq, True, BF16, cq=cq, ck=ck)
    w_out = w_out.astype(BF16)
    return _out_ln([o_d, o_f], [w_out[:GDN_WIDTH], w_out[GDN_WIDTH:]], xt, gain, bias, alpha, tm=256)


def _mla_layer(xt, positions, batch, seq, w_in, q_norm, kv_norm, w_uq, w_ukv, w_out, gain, bias, alpha):
    t = xt.shape[0]
    half = MLA_ROPE // 2
    o_kv, o_kr = MLA_Q_LORA, MLA_Q_LORA + MLA_KV_LORA

    def rope_cols(w):
        return jnp.concatenate([w, -w[:, half:], w[:, :half]], axis=1)

    w_kr = jnp.concatenate([jnp.zeros((w_in.shape[0], MLA_NOPE), w_in.dtype), rope_cols(w_in[:, o_kr:])], axis=1)
    ws = [w_in[:, :o_kv], w_in[:, o_kv:o_kr], w_kr]
    c_q, c_kv, kr_blk = _proj(xt, [w.astype(BF16) for w in ws], [F32, F32, F32], tm=256)

    dq = MLA_NOPE + MLA_ROPE
    wq3 = w_uq.reshape(MLA_Q_LORA, MLA_HEADS, dq)
    wq_blk = jnp.concatenate([wq3, -wq3[:, :, MLA_NOPE + half:], wq3[:, :, MLA_NOPE:MLA_NOPE + half]], axis=2)
    wq_blk = wq_blk.reshape(MLA_Q_LORA, MLA_HEADS * LANES).astype(BF16)
    wkv3 = w_ukv.reshape(MLA_KV_LORA, MLA_HEADS, MLA_NOPE + MLA_V)
    wk_blk = jnp.concatenate([wkv3[:, :, :MLA_NOPE], jnp.zeros((MLA_KV_LORA, MLA_HEADS, LANES - MLA_NOPE), w_ukv.dtype)], axis=2)
    wk_blk = wk_blk.reshape(MLA_KV_LORA, MLA_HEADS * LANES).astype(BF16)
    wv_blk = wkv3[:, :, MLA_NOPE:].reshape(MLA_KV_LORA, MLA_HEADS * MLA_V).astype(BF16)

    inv_freq = ROPE_THETA ** (-jnp.arange(0, MLA_ROPE, 2, dtype=F32) / MLA_ROPE)
    inv_row = jnp.concatenate([jnp.zeros((MLA_NOPE,), F32)] + [inv_freq] * 4).reshape(1, LANES)
    cos, sin = _rope_tables(positions.reshape(t, 1).astype(jnp.int32), inv_row)

    q_blk = _mla_q(c_q, q_norm.astype(F32).reshape(1, -1), wq_blk, cos, sin)
    k_blk, v_blk = _mla_kv(c_kv, kv_norm.astype(F32).reshape(1, -1), kr_blk, wk_blk, wv_blk, cos, sin)
    o = _attention(q_blk, 0, k_blk, 0, v_blk, 0, MLA_HEADS, batch, seq, False, BF16)
    return _out_ln([o], [w_out.astype(BF16)], xt, gain, bias, alpha, tm=256)


def kernel(x, positions, ln_gain, ln_bias, router_w, router_bias, moe_w_gate, moe_w_up, moe_w_down, hy_w_in, hy_conv_w,
           gdn_a_log, gdn_dt_bias, gdn_norm_w, fox_f_bias, hy_w_out, mla_w_in, mla_q_norm, mla_kv_norm, mla_w_uq,
           mla_w_ukv, mla_w_out):
    batch, seq, d = x.shape
    depth = ln_gain.shape[0]
    alpha = (2.0 * depth) ** 0.25
    xt = x.reshape(batch * seq, d)
    router_wt = router_w.astype(F32).T
    bias_col = router_bias.astype(F32).reshape(N_EXPERTS, 1)
    for layer in range(depth):
        j = layer // 2
        if layer % 2 == 0:
            xt = _delta_fox_layer(xt, batch, seq, hy_w_in[j], hy_conv_w[j], gdn_a_log[j], gdn_dt_bias[j], gdn_norm_w[j],
                                  fox_f_bias[j], hy_w_out[j], ln_gain[layer, 0], ln_bias[layer, 0], alpha)
        else:
            xt = _mla_layer(xt, positions, batch, seq, mla_w_in[j], mla_q_norm[j], mla_kv_norm[j], mla_w_uq[j],
                            mla_w_ukv[j], mla_w_out[j], ln_gain[layer, 0], ln_bias[layer, 0], alpha)
        xt = _moe_layer(xt, router_wt, bias_col, moe_w_gate, moe_w_up, moe_w_down, layer,
                        ln_gain[layer, 1], ln_bias[layer, 1], alpha)
    return xt.reshape(batch, seq, d)
```
